```python
import jax, jax.numpy as jnp
from jax import lax
import numpy as np

D_MODEL = 1024
BATCH = 16
SEQ = 4096
DEPTH = 1

N_META = 16
HEAD_DIM = 64
N_HEADS = D_MODEL // HEAD_DIM
N_KV_HEADS = N_HEADS // 4
GQA_GROUP = N_HEADS // N_KV_HEADS
WINDOW = 128
ATTN_BLOCK = 128
ROPE_THETA = 500000.0
ROPE_DIM = HEAD_DIM // 4
ML_HEADS = 4
ML_V_DIM = D_MODEL // ML_HEADS
ML_QK_DIM = ML_V_DIM // 2
ML_CHUNK = 64
CONV_WIDTH = 4
RMS_EPS = 1e-6
LN_EPS = 1e-6
NEG_BIG = -1e30

ATTN_WIDTH = N_HEADS * HEAD_DIM
KV_WIDTH = N_KV_HEADS * HEAD_DIM
ML_QK_WIDTH = ML_HEADS * ML_QK_DIM
ML_WIDTH = ML_HEADS * ML_V_DIM
SPLIT_SIZES = (ATTN_WIDTH, KV_WIDTH, KV_WIDTH, ATTN_WIDTH,
               2 * ML_QK_WIDTH, ML_WIDTH, ML_HEADS, ML_HEADS, ML_WIDTH, ML_WIDTH,
               D_MODEL, D_MODEL)
IN_WIDTH = sum(SPLIT_SIZES)
SPLIT_POINTS = tuple(int(c) for c in np.cumsum(SPLIT_SIZES)[:-1])

kernel_name = "hybrid_swa_sink_mlstm_gated_merge"


def _rmsnorm(x, g):
    xf = x.astype(jnp.float32)
    y = xf * lax.rsqrt(jnp.mean(xf * xf, axis=-1, keepdims=True) + RMS_EPS)
    return (y * g.astype(jnp.float32)).astype(x.dtype)


def _partial_rope(t, pos):
    half = ROPE_DIM // 2
    inv_freq = ROPE_THETA ** (-jnp.arange(0, ROPE_DIM, 2, dtype=jnp.float32) / ROPE_DIM)
    ang = pos[:, None] * inv_freq[None, :]
    cos = jnp.cos(ang)[None, :, None, :]
    sin = jnp.sin(ang)[None, :, None, :]
    tf = t.astype(jnp.float32)
    t1, t2, rest = tf[..., :half], tf[..., half:ROPE_DIM], tf[..., ROPE_DIM:]
    out = jnp.concatenate([t1 * cos - t2 * sin, t2 * cos + t1 * sin, rest], axis=-1)
    return out.astype(t.dtype)


def _sink_attention(q, k, v, mask, sink):
    s = jnp.einsum('bqhgd,bkhd->bhgqk', q, k).astype(jnp.float32)
    s = jnp.where(mask, s, NEG_BIG)
    sink_b = sink.astype(jnp.float32)[None, :, :, None, None]
    mx = jnp.maximum(jnp.max(s, axis=-1, keepdims=True), sink_b)
    p = jnp.exp(s - mx)
    denom = jnp.sum(p, axis=-1, keepdims=True) + jnp.exp(sink_b - mx)
    p = (p / denom).astype(v.dtype)
    return jnp.einsum('bhgqk,bkhd->bqhgd', p, v)


def _swa_sink_branch(q, k, v, sinks):
    B, L = q.shape[0], q.shape[1]
    S = L - N_META
    NB = S // ATTN_BLOCK
    pos = jnp.arange(L, dtype=jnp.float32)
    q = _partial_rope(q, pos) * (HEAD_DIM ** -0.5)
    k = _partial_rope(k, pos)
    q = q.reshape(B, L, N_KV_HEADS, GQA_GROUP, HEAD_DIM)
    sink = sinks.reshape(N_KV_HEADS, GQA_GROUP)
    qm, km, vm = q[:, :N_META], k[:, :N_META], v[:, :N_META]
    meta_mask = jnp.tril(jnp.ones((N_META, N_META), dtype=bool))
    out_meta = _sink_attention(qm, km, vm, meta_mask, sink)

    def blocks(t):
        return t[:, N_META:].reshape((B, NB, ATTN_BLOCK) + t.shape[2:])

    qb, kb, vb = blocks(q), blocks(k), blocks(v)

    def band_keys(tb, tm):
        prev = jnp.concatenate([jnp.zeros_like(tb[:, :1]), tb[:, :-1]], axis=1)
        meta = jnp.broadcast_to(tm[:, None], (B, NB) + tm.shape[1:])
        return jnp.concatenate([meta, prev, tb], axis=2).swapaxes(0, 1)

    kband, vband = band_keys(kb, km), band_keys(vb, vm)
    qi = jnp.arange(ATTN_BLOCK)[:, None]
    kj = jnp.arange(2 * ATTN_BLOCK)[None, :]
    rel = qi + ATTN_BLOCK - kj
    blk = jnp.arange(NB)[:, None, None]
    local = (rel >= 0) & (rel < WINDOW) & (blk * ATTN_BLOCK + kj - ATTN_BLOCK >= 0)
    mask = jnp.concatenate([jnp.ones((NB, ATTN_BLOCK, N_META), dtype=bool), local], axis=-1)
    out_blocks = lax.map(lambda a: _sink_attention(a[0], a[1], a[2], a[3], sink),
                         (qb.swapaxes(0, 1), kband, vband, mask))
    out_real = out_blocks.swapaxes(0, 1).reshape(B, S, ATTN_WIDTH)
    return jnp.concatenate([out_meta.reshape(B, N_META, ATTN_WIDTH), out_real], axis=1)


def _causal_conv(t, w, b):
    out = lax.conv_general_dilated(
        t, w[:, None, :].astype(t.dtype), window_strides=(1,),
        padding=((CONV_WIDTH - 1, 0),), dimension_numbers=('NWC', 'WIO', 'NWC'),
        feature_group_count=t.shape[-1])
    return out + b.astype(t.dtype)


def _mlstm_cell(q, k, v, i_pre, f_pre):
    B, L = q.shape[0], q.shape[1]
    f32 = jnp.float32
    pad = (-N_META) % ML_CHUNK
    q = q.astype(f32)
    k = k.astype(f32) * (ML_QK_DIM ** -0.5)
    v = v.astype(f32)
    li = i_pre.astype(f32)
    lf = jax.nn.log_sigmoid(f_pre.astype(f32))

    def padt(t, val):
        return jnp.pad(t, ((0, 0), (pad, 0)) + ((0, 0),) * (t.ndim - 2), constant_values=val)

    q, k, v, lf = padt(q, 0.0), padt(k, 0.0), padt(v, 0.0), padt(lf, 0.0)
    li = padt(li, NEG_BIG)
    NC = (L + pad) // ML_CHUNK

    def chunks(t):
        t = t.reshape((B, NC, ML_CHUNK) + t.shape[2:])
        return t.transpose((1, 0, 3, 2) + tuple(range(4, t.ndim)))

    xs = (chunks(q), chunks(k), chunks(v), chunks(li), chunks(lf))
    causal = jnp.tril(jnp.ones((ML_CHUNK, ML_CHUNK), dtype=bool))

    def body(carry, inp):
        C, n, m = carry
        qc, kc, vc, lic, lfc = inp
        bcum = jnp.cumsum(lfc, axis=-1)
        Dm = jnp.where(causal, bcum[..., :, None] - bcum[..., None, :] + lic[..., None, :], NEG_BIG)
        inter = bcum + m[..., None]
        m_row = jnp.maximum(inter, jnp.max(Dm, axis=-1))
        w_inter = jnp.exp(inter - m_row)
        sc = jnp.einsum('bhtd,bhsd->bhts', qc, kc) * jnp.exp(Dm - m_row[..., None])
        num = w_inter[..., None] * jnp.einsum('bhtd,bhvd->bhtv', qc, C) + jnp.einsum('bhts,bhsv->bhtv', sc, vc)
        den = w_inter * jnp.einsum('bhtd,bhd->bht', qc, n) + jnp.sum(sc, axis=-1)
        h = num / jnp.maximum(jnp.abs(den), jnp.exp(-m_row))[..., None]
        b_last = bcum[..., -1]
        a = b_last[..., None] - bcum + lic
        m_new = jnp.maximum(b_last + m, jnp.max(a, axis=-1))
        w_c = jnp.exp(b_last + m - m_new)
        w_a = jnp.exp(a - m_new[..., None])
        C = w_c[..., None, None] * C + jnp.einsum('bhs,bhsv,bhsd->bhvd', w_a, vc, kc)
        n = w_c[..., None] * n + jnp.einsum('bhs,bhsd->bhd', w_a, kc)
        return (C, n, m_new), h

    init = (jnp.zeros((B, ML_HEADS, ML_V_DIM, ML_QK_DIM), f32),
            jnp.zeros((B, ML_HEADS, ML_QK_DIM), f32),
            jnp.zeros((B, ML_HEADS), f32))
    _, hs = lax.scan(body, init, xs)
    hs = hs.transpose(1, 0, 3, 2, 4).reshape(B, NC * ML_CHUNK, ML_HEADS, ML_V_DIM)
    return hs[:, pad:]


def _head_layernorm(h, g):
    mu = jnp.mean(h, axis=-1, keepdims=True)
    hc = h - mu
    y = hc * lax.rsqrt(jnp.mean(hc * hc, axis=-1, keepdims=True) + LN_EPS)
    y = y * g.astype(jnp.float32).reshape(ML_HEADS, ML_V_DIM)
    return y.reshape(h.shape[0], h.shape[1], ML_WIDTH)


def _hybrid_mixer(u, w_in, attn_sinks, conv_w, conv_b, gate_bias, head_norm,
                  w_attn_out, w_mlstm_out, w_out):
    B, L, _ = u.shape
    proj = u @ w_in.astype(u.dtype)
    (a_q, a_k, a_v, a_z, m_qk, m_v, m_i, m_f, m_o, m_z, g_a, g_m) = jnp.split(proj, SPLIT_POINTS, axis=-1)
    att = _swa_sink_branch(a_q.reshape(B, L, N_HEADS, HEAD_DIM),
                           a_k.reshape(B, L, N_KV_HEADS, HEAD_DIM),
                           a_v.reshape(B, L, N_KV_HEADS, HEAD_DIM), attn_sinks)
    y_a = (att * jax.nn.silu(a_z)) @ w_attn_out.astype(u.dtype)
    m_qk = jax.nn.silu(_causal_conv(m_qk, conv_w, conv_b))
    m_q, m_k = jnp.split(m_qk, 2, axis=-1)
    cell = _mlstm_cell(m_q.reshape(B, L, ML_HEADS, ML_QK_DIM),
                       m_k.reshape(B, L, ML_HEADS, ML_QK_DIM),
                       m_v.reshape(B, L, ML_HEADS, ML_V_DIM),
                       m_i + gate_bias[:ML_HEADS].astype(u.dtype),
                       m_f + gate_bias[ML_HEADS:].astype(u.dtype))
    o_gate = jax.nn.sigmoid(m_o.astype(jnp.float32)).reshape(B, L, ML_HEADS, ML_V_DIM)
    hm = _head_layernorm(o_gate * cell, head_norm).astype(u.dtype)
    y_m = (hm * jax.nn.silu(m_z)) @ w_mlstm_out.astype(u.dtype)
    merged = jax.nn.sigmoid(g_a) * y_a + jax.nn.sigmoid(g_m) * y_m
    return merged @ w_out.astype(u.dtype)


def setup_inputs(seed: int = 0) -> dict:
    key = jax.random.key(seed)
    ks = jax.random.split(key, 16)
    nrm = jax.random.normal
    x = nrm(ks[0], (BATCH, SEQ, D_MODEL), jnp.float32)
    meta_tokens = nrm(ks[1], (N_META, D_MODEL), jnp.float32)
    norm_pre = 1.0 + 0.01 * nrm(ks[2], (DEPTH, D_MODEL), jnp.float32)
    w_in = nrm(ks[3], (DEPTH, D_MODEL, IN_WIDTH), jnp.float32) * D_MODEL ** -0.5
    attn_sinks = 0.5 * nrm(ks[4], (DEPTH, N_HEADS), jnp.float32)
    conv_w = nrm(ks[5], (DEPTH, CONV_WIDTH, 2 * ML_QK_WIDTH), jnp.float32) * CONV_WIDTH ** -0.5
    conv_b = 0.01 * nrm(ks[6], (DEPTH, 2 * ML_QK_WIDTH), jnp.float32)
    i_bias = 0.1 * nrm(ks[7], (DEPTH, ML_HEADS), jnp.float32)
    f_bias = 3.0 + 3.0 * jax.random.uniform(ks[8], (DEPTH, ML_HEADS), jnp.float32)
    mlstm_gate_bias = jnp.concatenate([i_bias, f_bias], axis=-1)
    mlstm_head_norm = 1.0 + 0.01 * nrm(ks[9], (DEPTH, ML_WIDTH), jnp.float32)
    w_attn_out = nrm(ks[10], (DEPTH, ATTN_WIDTH, D_MODEL), jnp.float32) * ATTN_WIDTH ** -0.5
    w_mlstm_out = nrm(ks[11], (DEPTH, ML_WIDTH, D_MODEL), jnp.float32) * ML_WIDTH ** -0.5
    w_out = nrm(ks[12], (DEPTH, D_MODEL, D_MODEL), jnp.float32) * D_MODEL ** -0.5
    norm_post = 1.0 + 0.01 * nrm(ks[13], (DEPTH, D_MODEL), jnp.float32)
    return {"x": x, "meta_tokens": meta_tokens, "norm_pre": norm_pre, "w_in": w_in,
            "attn_sinks": attn_sinks, "conv_w": conv_w, "conv_b": conv_b,
            "mlstm_gate_bias": mlstm_gate_bias, "mlstm_head_norm": mlstm_head_norm,
            "w_attn_out": w_attn_out, "w_mlstm_out": w_mlstm_out, "w_out": w_out,
            "norm_post": norm_post}


def reference(x, meta_tokens, norm_pre, w_in, attn_sinks, conv_w, conv_b, mlstm_gate_bias,
              mlstm_head_norm, w_attn_out, w_mlstm_out, w_out, norm_post):
    B = x.shape[0]
    meta = jnp.broadcast_to(meta_tokens.astype(x.dtype)[None], (B,) + meta_tokens.shape)
    h = jnp.concatenate([meta, x], axis=1)
    for layer in range(DEPTH):
        u = _rmsnorm(h, norm_pre[layer])
        y = _hybrid_mixer(u, w_in[layer], attn_sinks[layer], conv_w[layer], conv_b[layer],
                          mlstm_gate_bias[layer], mlstm_head_norm[layer],
                          w_attn_out[layer], w_mlstm_out[layer], w_out[layer])
        h = h + _rmsnorm(y, norm_post[layer])
    return h[:, N_META:]
```

```python
import functools

import numpy as np
import jax
import jax.numpy as jnp
from jax import lax
from jax.experimental import pallas as pl
from jax.experimental.pallas import tpu as pltpu

F32 = jnp.float32
BF16 = jnp.bfloat16

D_MODEL = 1024
N_META = 16
HEAD_DIM = 64
N_HEADS = 16
N_KV = 4
ATTN_BLOCK = 128
ROPE_DIM = 16
ROPE_THETA = 500000.0
ML_HEADS = 4
ML_DV = 256
ML_DK = 128
ML_EXT = ML_DV + 128
CONV_W = 4
CHUNK = 128
RMS_EPS = 1e-6
LN_EPS = 1e-6
NEG_BIG = -1e30
LANES = 128
CONV_HDR = 8
SEQ_TILE = 512
VMEM_LIMIT = 56 * 1024 * 1024


def _dot(a, b):
    return jnp.dot(a, b, preferred_element_type=F32)


def _dot_nt(a, b):
    return lax.dot_general(a, b, (((1,), (1,)), ((), ())), preferred_element_type=F32)


def _sigmoid(x):
    return 1.0 / (1.0 + jnp.exp(-x))


def _silu(x):
    return x * _sigmoid(x)


def _log_sigmoid(x):
    return jnp.minimum(x, 0.0) - jnp.log(1.0 + jnp.exp(-jnp.abs(x)))


def _rmsnorm(xf, g):
    ms = jnp.mean(xf * xf, axis=-1, keepdims=True)
    return xf * lax.rsqrt(ms + RMS_EPS) * g


def _rope(blk, c_t, s1, s2):
    return blk * c_t + pltpu.roll(blk, LANES - 8, 1) * s1 + pltpu.roll(blk, 8, 1) * s2


def _causal_conv_silu(cs_ref, w_ref, b_ref, rows, cols):
    acc = b_ref[:, cols]
    for tap in range(CONV_W):
        off = CONV_HDR - (CONV_W - 1) + tap
        acc = acc + w_ref[tap:tap + 1, cols] * cs_ref[off:off + rows, cols]
    return _silu(acc)


def _gate_rows(gcol):
    grow = gcol.T[0:8, :]
    lane = lax.broadcasted_iota(jnp.int32, grow.shape, 1)
    brow = grow
    sh = 1
    while sh < CHUNK:
        brow = brow + jnp.where(lane >= sh, pltpu.roll(brow, sh, 1), 0.0)
        sh *= 2
    pad = jnp.zeros((LANES - 8, CHUNK), F32)
    bcol = jnp.concatenate([brow, pad], axis=0).T
    return grow, brow, bcol


def _mlstm_chunk(q, k, vx, li_col, li_row, b_col, b_row, c_ext, m_prev, causal):
    n = q.shape[0]
    dm = jnp.where(causal, b_col - b_row + li_row, NEG_BIG)
    inter = b_col + m_prev
    m_row = jnp.maximum(inter, jnp.max(dm, axis=-1, keepdims=True))
    w_inter = jnp.exp(inter - m_row)
    sc = _dot_nt(q, k.astype(BF16)) * jnp.exp(dm - m_row)
    he = w_inter * _dot(q, c_ext.astype(BF16)) + _dot(sc.astype(BF16), vx)
    den = he[:, ML_DV:ML_DV + 1]
    h = he[:, :ML_DV] / jnp.maximum(jnp.abs(den), jnp.exp(-m_row))
    b_last = b_col[n - 1:n, :]
    a_col = b_last - b_col + li_col
    m_new = jnp.maximum(b_last + m_prev, jnp.max(a_col, axis=0, keepdims=True))
    w_c = jnp.exp(b_last + m_prev - m_new)
    w_a = jnp.exp(a_col - m_new)
    kw_t = (k * w_a).T.astype(BF16)
    c_new = w_c * c_ext + _dot(kw_t, vx)
    return h, c_new, m_new


def _mlstm_chunk_all_heads(r0, g_ref, q_ref, k_ref, vx_ref, c_ref, m_ref, hs_ref):
    gcol = g_ref[pl.ds(r0, CHUNK), :]
    grow, brow, bcol = _gate_rows(gcol)
    row = lax.broadcasted_iota(jnp.int32, (CHUNK, CHUNK), 0)
    col = lax.broadcasted_iota(jnp.int32, (CHUNK, CHUNK), 1)
    causal = row >= col
    for h in range(ML_HEADS):
        q = q_ref[pl.ds(r0, CHUNK), ML_DK * h:ML_DK * (h + 1)]
        k = k_ref[pl.ds(r0, CHUNK), ML_DK * h:ML_DK * (h + 1)]
        vx = vx_ref[h, pl.ds(r0, CHUNK), :]
        hh, c_new, m_new = _mlstm_chunk(
            q, k, vx,
            gcol[:, h:h + 1], grow[h:h + 1, :],
            bcol[:, ML_HEADS + h:ML_HEADS + h + 1], brow[ML_HEADS + h:ML_HEADS + h + 1, :],
            c_ref[h], m_ref[h:h + 1, 0:1], causal)
        if hs_ref is not None:
            hs_ref[pl.ds(r0, CHUNK), ML_DV * h:ML_DV * (h + 1)] = hh
        c_ref[h] = c_new
        m_ref[h:h + 1, :] = jnp.broadcast_to(m_new, (1, LANES))


def _gates(gi):
    lane = lax.broadcasted_iota(jnp.int32, gi.shape, 1)
    return jnp.where(lane < ML_HEADS, gi, jnp.where(lane < 2 * ML_HEADS, _log_sigmoid(gi), 0.0))


def _store_values_ext(v, vx_ref):
    rows = v.shape[0]
    lane = lax.broadcasted_iota(jnp.int32, (rows, LANES), 1)
    one_col = jnp.where(lane == 0, 1.0, 0.0).astype(BF16)
    for h in range(ML_HEADS):
        vx_ref[h, :, 0:ML_DV] = v[:, ML_DV * h:ML_DV * (h + 1)].astype(BF16)
        vx_ref[h, :, ML_DV:ML_EXT] = one_col


def _meta_kernel(xm_ref, gpre_ref, wk_ref, wv_ref, wmqk_ref, wmv_ref, wif_ref,
                 convw_ref, convb_ref, gbias_ref, rc_ref, rs1_ref, rs2_ref,
                 kmeta_ref, vmeta_ref, tail_ref, c0_ref, m0_ref,
                 cs_s, q_s, k_s, vx_s, g_s):
    rows = xm_ref.shape[0]
    u = _rmsnorm(xm_ref[...], gpre_ref[...]).astype(BF16)
    valid = lax.broadcasted_iota(jnp.int32, (rows, 1), 0) >= rows - N_META

    kc = _dot(u, wk_ref[...])
    for cb in range(kc.shape[1] // LANES):
        sl = slice(LANES * cb, LANES * (cb + 1))
        kmeta_ref[:, sl] = _rope(kc[:, sl], rc_ref[...], rs1_ref[...], rs2_ref[...])
    vmeta_ref[...] = _dot(u, wv_ref[...])

    cs_s[0:CONV_HDR, :] = jnp.zeros((CONV_HDR, D_MODEL), F32)
    cs_s[CONV_HDR:, :] = _dot(u, wmqk_ref[...])
    tail_ref[...] = cs_s[rows:rows + CONV_HDR, :]
    half = D_MODEL // 2
    for cc in range(4):
        cols = slice(256 * cc, 256 * (cc + 1))
        act = jnp.where(valid, _causal_conv_silu(cs_s, convw_ref, convb_ref, rows, cols), 0.0)
        if cc < 2:
            q_s[:, cols] = act.astype(BF16)
        else:
            k_s[:, 256 * cc - half:256 * (cc + 1) - half] = act * (ML_DK ** -0.5)

    _store_values_ext(jnp.where(valid, _dot(u, wmv_ref[...]), 0.0), vx_s)

    gi = _dot(u, wif_ref[...]) + gbias_ref[...]
    lane = lax.broadcasted_iota(jnp.int32, gi.shape, 1)
    pad_gate = jnp.where(lane < ML_HEADS, NEG_BIG, 0.0)
    g_s[...] = jnp.where(valid, _gates(gi), pad_gate)

    c0_ref[...] = jnp.zeros(c0_ref.shape, F32)
    m0_ref[...] = jnp.zeros(m0_ref.shape, F32)
    _mlstm_chunk_all_heads(0, g_s, q_s, k_s, vx_s, c0_ref, m0_ref, None)


def _attn_kernel(x_ref, gpre_ref, wq_ref, wk_ref, wv_ref, wz_ref, wga_ref, wao_ref,
                 rc_ref, rs1_ref, rs2_ref, kmp_ref, vmp_ref, lb_ref, lb0_ref, mb_ref, sinks_ref,
                 out_ref,
                 u_s, q_s, kk_s, vv_s, a_s, gated_s):
    j = pl.program_id(1)
    tile = x_ref.shape[1]
    nblk = tile // ATTN_BLOCK
    pairs = N_HEADS // 2

    u_s[...] = _rmsnorm(x_ref[0], gpre_ref[...]).astype(BF16)

    c_t, s1, s2 = rc_ref[...], rs1_ref[...], rs2_ref[...]
    for half in range(2):
        qc = _dot(u_s[...], wq_ref[:, 512 * half:512 * (half + 1)])
        for cc in range(4):
            blk = _rope(qc[:, LANES * cc:LANES * (cc + 1)], c_t, s1, s2)
            q_s[4 * half + cc] = (blk * (HEAD_DIM ** -0.5)).astype(BF16)

    @pl.when(j == 0)
    def _():
        zeros = jnp.zeros((N_KV, 2, ATTN_BLOCK, LANES), BF16)
        kk_s[:, :, 0:ATTN_BLOCK, :] = zeros
        vv_s[:, :, 0:ATTN_BLOCK, :] = zeros

    kc = _dot(u_s[...], wk_ref[...])
    vc = _dot(u_s[...], wv_ref[...])
    low = lax.broadcasted_iota(jnp.int32, (tile, LANES), 1) < HEAD_DIM
    for cb in range(N_KV // 2):
        sl = slice(LANES * cb, LANES * (cb + 1))
        for src, dst, rot in ((kc, kk_s, True), (vc, vv_s, False)):
            blk = src[:, sl]
            if rot:
                blk = _rope(blk, c_t, s1, s2)
            swapped = pltpu.roll(blk, HEAD_DIM, 1)
            dst[2 * cb, 0, ATTN_BLOCK:, :] = jnp.where(low, blk, 0.0).astype(BF16)
            dst[2 * cb, 1, ATTN_BLOCK:, :] = jnp.where(low, 0.0, swapped).astype(BF16)
            dst[2 * cb + 1, 0, ATTN_BLOCK:, :] = jnp.where(low, swapped, 0.0).astype(BF16)
            dst[2 * cb + 1, 1, ATTN_BLOCK:, :] = jnp.where(low, 0.0, blk).astype(BF16)

    lane_low = lax.broadcasted_iota(jnp.int32, (ATTN_BLOCK, LANES), 1) < HEAD_DIM

    def block_body(i, carry):
        r0 = pl.multiple_of(i * ATTN_BLOCK, ATTN_BLOCK)
        first = jnp.logical_and(j == 0, i == 0)
        lbias = jnp.where(first, lb0_ref[...], lb_ref[...])
        for c in range(pairs):
            g = c // 2
            qp = q_s[c, pl.ds(r0, ATTN_BLOCK), :]
            sm = _dot_nt(qp, kmp_ref[g])
            probs, rinv, pm = [], [], None
            for hh in range(2):
                s = _dot_nt(qp, kk_s[g, hh, pl.ds(r0, 2 * ATTN_BLOCK), :]) + lbias
                smh = sm + mb_ref[hh:hh + 1, :]
                sink = sinks_ref[2 * c + hh]
                cm = jnp.maximum(jnp.maximum(s[:, :LANES], s[:, LANES:]), smh)
                mx = jnp.maximum(jnp.max(cm, axis=-1, keepdims=True), sink)
                p = jnp.exp(s - mx)
                pmh = jnp.exp(smh - mx)
                den = (jnp.sum(p[:, :LANES] + p[:, LANES:] + pmh, axis=-1, keepdims=True)
                       + jnp.exp(sink - mx))
                probs.append(p.astype(BF16))
                rinv.append(1.0 / den)
                pm = pmh if pm is None else pm + pmh
            vcat = jnp.concatenate([vv_s[g, 0, pl.ds(r0, 2 * ATTN_BLOCK), :],
                                    vv_s[g, 1, pl.ds(r0, 2 * ATTN_BLOCK), :]], axis=0)
            o = _dot(jnp.concatenate(probs, axis=1), vcat) + _dot(pm.astype(BF16), vmp_ref[g])
            a_s[c, pl.ds(r0, ATTN_BLOCK), :] = o * jnp.where(lane_low, rinv[0], rinv[1])
        return carry

    lax.fori_loop(0, nblk, block_body, 0)

    kk_s[:, :, 0:ATTN_BLOCK, :] = kk_s[:, :, tile:tile + ATTN_BLOCK, :]
    vv_s[:, :, 0:ATTN_BLOCK, :] = vv_s[:, :, tile:tile + ATTN_BLOCK, :]

    for g4 in range(4):
        cols = slice(256 * g4, 256 * (g4 + 1))
        att = jnp.concatenate([a_s[2 * g4], a_s[2 * g4 + 1]], axis=1)
        gated_s[:, cols] = (att * _silu(_dot(u_s[...], wz_ref[:, cols]))).astype(BF16)
    for half in range(2):
        cols = slice(512 * half, 512 * (half + 1))
        y = _dot(gated_s[...], wao_ref[:, cols])
        ga = _dot(u_s[...], wga_ref[:, cols])
        out_ref[0, :, cols] = (_sigmoid(ga) * y).astype(out_ref.dtype)


def _mlstm_kernel(x_ref, ya_ref, gpre_ref, gpost_ref, wmqk_ref, wmv_ref, wif_ref, wmo_ref, wmz_ref,
                  wgm_ref, wmlo_ref, wout_ref, convw_ref, convb_ref, gbias_ref, hnorm_ref,
                  tail0_ref, c0_ref, m0_ref,
                  out_ref,
                  u_s, cs_s, q_s, k_s, vx_s, g_s, hs_s, hm_s, mg_s, c_s, m_s):
    j = pl.program_id(1)
    tile = x_ref.shape[1]

    u_s[...] = _rmsnorm(x_ref[0], gpre_ref[...]).astype(BF16)

    @pl.when(j == 0)
    def _():
        cs_s[0:CONV_HDR, :] = tail0_ref[...]
        c_s[...] = c0_ref[...]
        m_s[...] = m0_ref[...]

    for half in range(2):
        cols = slice(512 * half, 512 * (half + 1))
        cs_s[CONV_HDR:, cols] = _dot(u_s[...], wmqk_ref[:, cols])
    hw = D_MODEL // 2
    for cc in range(4):
        cols = slice(256 * cc, 256 * (cc + 1))
        act = _causal_conv_silu(cs_s, convw_ref, convb_ref, tile, cols)
        if cc < 2:
            q_s[:, cols] = act.astype(BF16)
        else:
            k_s[:, 256 * cc - hw:256 * (cc + 1) - hw] = act * (ML_DK ** -0.5)
    cs_s[0:CONV_HDR, :] = cs_s[tile:tile + CONV_HDR, :]

    _store_values_ext(_dot(u_s[...], wmv_ref[...]), vx_s)
    g_s[...] = _gates(_dot(u_s[...], wif_ref[...]) + gbias_ref[...])

    def chunk_body(c, carry):
        _mlstm_chunk_all_heads(pl.multiple_of(c * CHUNK, CHUNK), g_s, q_s, k_s, vx_s, c_s, m_s, hs_s)
        return carry

    lax.fori_loop(0, tile // CHUNK, chunk_body, 0)

    for h in range(ML_HEADS):
        cols = slice(ML_DV * h, ML_DV * (h + 1))
        hg = _sigmoid(_dot(u_s[...], wmo_ref[:, cols])) * hs_s[:, cols]
        hc = hg - jnp.mean(hg, axis=-1, keepdims=True)
        y = hc * lax.rsqrt(jnp.mean(hc * hc, axis=-1, keepdims=True) + LN_EPS) * hnorm_ref[:, cols]
        hm_s[:, cols] = (y * _silu(_dot(u_s[...], wmz_ref[:, cols]))).astype(BF16)
    for half in range(2):
        cols = slice(512 * half, 512 * (half + 1))
        ym = _dot(hm_s[...], wmlo_ref[:, cols])
        gm = _sigmoid(_dot(u_s[...], wgm_ref[:, cols]))
        mg_s[:, cols] = (ya_ref[0, :, cols].astype(F32) + gm * ym).astype(BF16)
    o = _dot(mg_s[...], wout_ref[...])
    out_ref[0] = x_ref[0] + _rmsnorm(o, gpost_ref[...])


def _rope_tables(pos):
    half = ROPE_DIM // 2
    inv_freq = ROPE_THETA ** (-jnp.arange(0, ROPE_DIM, 2, dtype=F32) / ROPE_DIM)
    ang = pos[:, None] * inv_freq[None, :]
    cos, sin = jnp.cos(ang), jnp.sin(ang)
    ones = jnp.ones((pos.shape[0], HEAD_DIM - ROPE_DIM), F32)
    zeros_h = jnp.zeros((pos.shape[0], half), F32)
    zeros_r = jnp.zeros_like(ones)
    c_head = jnp.concatenate([cos, cos, ones], axis=1)
    s1_head = jnp.concatenate([-sin, zeros_h, zeros_r], axis=1)
    s2_head = jnp.concatenate([zeros_h, sin, zeros_r], axis=1)
    rep = LANES // HEAD_DIM
    return (jnp.tile(c_head, (1, rep)), jnp.tile(s1_head, (1, rep)), jnp.tile(s2_head, (1, rep)))


def _band_bias():
    qi = np.arange(ATTN_BLOCK)[:, None]
    kj = np.arange(2 * ATTN_BLOCK)[None, :]
    vis = (kj > qi) & (kj <= qi + ATTN_BLOCK)
    lb = np.where(vis, 0.0, NEG_BIG).astype(np.float32)
    lb0 = np.where(vis & (kj >= ATTN_BLOCK), 0.0, NEG_BIG).astype(np.float32)
    lane = np.arange(LANES)[None, :]
    mb = np.full((8, LANES), NEG_BIG, np.float32)
    mb[0] = np.where(lane < N_META, 0.0, NEG_BIG)
    mb[1] = np.where((lane >= N_META) & (lane < 2 * N_META), 0.0, NEG_BIG)
    return jnp.asarray(lb), jnp.asarray(lb0), jnp.asarray(mb)


def _pair_slabs(t):
    tg = t.reshape(N_META, N_KV, HEAD_DIM).transpose(1, 0, 2)
    z = jnp.zeros_like(tg)
    top = jnp.concatenate([tg, z], axis=2)
    bot = jnp.concatenate([z, tg], axis=2)
    pad = jnp.zeros((N_KV, LANES - 2 * N_META, LANES), t.dtype)
    return jnp.concatenate([top, bot, pad], axis=1).astype(BF16)


def _full(shape, single=True):
    nd = len(shape)
    kw = {"pipeline_mode": pl.Buffered(1)} if single else {}
    return pl.BlockSpec(shape, lambda *_: (0,) * nd, **kw)


def kernel(x, meta_tokens, norm_pre, w_in, attn_sinks, conv_w, conv_b, mlstm_gate_bias, mlstm_head_norm,
           w_attn_out, w_mlstm_out, w_out, norm_post):
    B, S, D = x.shape
    assert D == D_MODEL and w_in.shape[0] == 1 and meta_tokens.shape == (N_META, D_MODEL)
    tile = min(SEQ_TILE, S)
    assert S % tile == 0 and tile % CHUNK == 0 and tile % ATTN_BLOCK == 0
    nt = S // tile

    w = w_in[0].astype(BF16)
    o = 0
    parts = {}
    for name, width in (("q", 1024), ("k", 256), ("v", 256), ("z", 1024), ("mqk", 1024), ("mv", 1024),
                        ("mi", 4), ("mf", 4), ("mo", 1024), ("mz", 1024), ("ga", 1024), ("gm", 1024)):
        parts[name] = w[:, o:o + width]
        o += width
    wif = jnp.concatenate([parts["mi"], parts["mf"],
                           jnp.zeros((D, LANES - 2 * ML_HEADS), BF16)], axis=1)
    gbias = jnp.concatenate([mlstm_gate_bias[0].astype(F32),
                             jnp.zeros((LANES - 2 * ML_HEADS,), F32)])[None, :]
    gpre = norm_pre[0].astype(F32)[None, :]
    gpost = norm_post[0].astype(F32)[None, :]
    convw = conv_w[0].astype(F32)
    convb = conv_b[0].astype(F32)[None, :]
    hnorm = mlstm_head_norm[0].astype(F32)[None, :]
    wao = w_attn_out[0].astype(BF16)
    wmlo = w_mlstm_out[0].astype(BF16)
    wout = w_out[0].astype(BF16)
    sinks = attn_sinks[0].astype(F32)

    xm = jnp.concatenate([jnp.zeros((CHUNK - N_META, D), F32), meta_tokens.astype(F32)], axis=0)
    mpos = jnp.maximum(jnp.arange(CHUNK, dtype=F32) - (CHUNK - N_META), 0.0)
    mrc, mrs1, mrs2 = _rope_tables(mpos)
    kv_w = N_KV * HEAD_DIM
    meta_out = pl.pallas_call(
        _meta_kernel,
        out_shape=(jax.ShapeDtypeStruct((CHUNK, kv_w), F32), jax.ShapeDtypeStruct((CHUNK, kv_w), F32),
                   jax.ShapeDtypeStruct((CONV_HDR, D), F32),
                   jax.ShapeDtypeStruct((ML_HEADS, ML_DK, ML_EXT), F32),
                   jax.ShapeDtypeStruct((8, LANES), F32)),
        scratch_shapes=[pltpu.VMEM((CONV_HDR + CHUNK, D), F32), pltpu.VMEM((CHUNK, D // 2), BF16),
                        pltpu.VMEM((CHUNK, D // 2), F32), pltpu.VMEM((ML_HEADS, CHUNK, ML_EXT), BF16),
                        pltpu.VMEM((CHUNK, LANES), F32)],
        compiler_params=pltpu.CompilerParams(vmem_limit_bytes=VMEM_LIMIT),
        name="meta_tokens",
    )(xm, gpre, parts["k"], parts["v"], parts["mqk"], parts["mv"], wif, convw, convb, gbias,
      mrc, mrs1, mrs2)
    kmeta, vmeta, tail0, c0, m0 = meta_out
    kmp = _pair_slabs(kmeta[CHUNK - N_META:])
    vmp = _pair_slabs(vmeta[CHUNK - N_META:])

    rc, rs1, rs2 = _rope_tables(jnp.arange(S, dtype=F32) + N_META)
    lb, lb0, mb = _band_bias()
    x_spec = pl.BlockSpec((1, tile, D), lambda b, j: (b, j, 0))
    rope_spec = pl.BlockSpec((tile, LANES), lambda b, j: (j, 0))
    seq_params = pltpu.CompilerParams(dimension_semantics=("arbitrary", "arbitrary"),
                                      vmem_limit_bytes=VMEM_LIMIT)
    ya = pl.pallas_call(
        _attn_kernel,
        grid=(B, nt),
        in_specs=[x_spec, _full((1, D)),
                  _full((D, 1024)), _full((D, kv_w)), _full((D, kv_w)), _full((D, 1024)),
                  _full((D, 1024)), _full((D, D)),
                  rope_spec, rope_spec, rope_spec,
                  _full((N_KV, LANES, LANES)), _full((N_KV, LANES, LANES)),
                  _full((ATTN_BLOCK, 2 * ATTN_BLOCK)), _full((ATTN_BLOCK, 2 * ATTN_BLOCK)),
                  _full((8, LANES)),
                  pl.BlockSpec(memory_space=pltpu.SMEM)],
        out_specs=pl.BlockSpec((1, tile, D), lambda b, j: (b, j, 0)),
        out_shape=jax.ShapeDtypeStruct((B, S, D), BF16),
        scratch_shapes=[pltpu.VMEM((tile, D), BF16),
                        pltpu.VMEM((N_HEADS // 2, tile, LANES), BF16),
                        pltpu.VMEM((N_KV, 2, ATTN_BLOCK + tile, LANES), BF16),
                        pltpu.VMEM((N_KV, 2, ATTN_BLOCK + tile, LANES), BF16),
                        pltpu.VMEM((N_HEADS // 2, tile, LANES), F32),
                        pltpu.VMEM((tile, D), BF16)],
        compiler_params=seq_params,
        name="attn_branch",
    )(x, gpre, parts["q"], parts["k"], parts["v"], parts["z"], parts["ga"], wao,
      rc, rs1, rs2, kmp, vmp, lb, lb0, mb, sinks)

    out = pl.pallas_call(
        _mlstm_kernel,
        grid=(B, nt),
        in_specs=[x_spec, pl.BlockSpec((1, tile, D), lambda b, j: (b, j, 0)),
                  _full((1, D)), _full((1, D)),
                  _full((D, 1024)), _full((D, 1024)), _full((D, LANES)), _full((D, 1024)),
                  _full((D, 1024)), _full((D, 1024)), _full((D, D)), _full((D, D)),
                  _full((CONV_W, D)), _full((1, D)), _full((1, LANES)), _full((1, D)),
                  _full((CONV_HDR, D)), _full((ML_HEADS, ML_DK, ML_EXT)), _full((8, LANES))],
        out_specs=pl.BlockSpec((1, tile, D), lambda b, j: (b, j, 0)),
        out_shape=jax.ShapeDtypeStruct((B, S, D), x.dtype),
        scratch_shapes=[pltpu.VMEM((tile, D), BF16),
                        pltpu.VMEM((CONV_HDR + tile, D), F32),
                        pltpu.VMEM((tile, D // 2), BF16),
                        pltpu.VMEM((tile, D // 2), F32),
                        pltpu.VMEM((ML_HEADS, tile, ML_EXT), BF16),
                        pltpu.VMEM((tile, LANES), F32),
                        pltpu.VMEM((tile, D), F32),
                        pltpu.VMEM((tile, D), BF16),
                        pltpu.VMEM((tile, D), BF16),
                        pltpu.VMEM((ML_HEADS, ML_DK, ML_EXT), F32),
                        pltpu.VMEM((8, LANES), F32)],
        compiler_params=seq_params,
        name="mlstm_merge_out",
    )(x, ya, gpre, gpost, parts["mqk"], parts["mv"], wif, parts["mo"], parts["mz"], parts["gm"],
      wmlo, wout, convw, convb, gbias, hnorm, tail0, c0, m0)
    return out
```

```python
import numpy as np
import jax
import jax.numpy as jnp
from jax import lax
from jax.experimental import pallas as pl
from jax.experimental.pallas import tpu as pltpu

F32 = jnp.float32
BF16 = jnp.bfloat16

D_MODEL = 1024
N_META = 16
HEAD_DIM = 64
N_HEADS = 16
N_KV = 4
ATTN_BLOCK = 128
BAND = N_META + 2 * ATTN_BLOCK
ROPE_DIM = 16
ROPE_THETA = 500000.0
ML_HEADS = 4
ML_DV = 256
ML_DK = 128
ML_EXT = ML_DV + 16
CONV_W = 4
CHUNK = 128
RMS_EPS = 1e-6
LN_EPS = 1e-6
NEG_BIG = -1e30
LANES = 128
CONV_HDR = 8
SEQ_TILE = 512
VMEM_LIMIT = 56 * 1024 * 1024


def _dot(a, b):
    return jnp.dot(a, b, preferred_element_type=F32)


def _dot_nt(a, b):
    return lax.dot_general(a, b, (((1,), (1,)), ((), ())), preferred_element_type=F32)


def _sigmoid(x):
    return 1.0 / (1.0 + jnp.exp(-x))


def _silu(x):
    return x * _sigmoid(x)


def _log_sigmoid(x):
    return jnp.minimum(x, 0.0) - jnp.log(1.0 + jnp.exp(-jnp.abs(x)))


def _rmsnorm(xf, g):
    ms = jnp.mean(xf * xf, axis=-1, keepdims=True)
    return xf * lax.rsqrt(ms + RMS_EPS) * g


def _rope(blk, c_t, s1, s2):
    return blk * c_t + pltpu.roll(blk, LANES - 8, 1) * s1 + pltpu.roll(blk, 8, 1) * s2


def _causal_conv_silu(cs_ref, w_ref, b_ref, rows, cols):
    acc = b_ref[:, cols]
    for tap in range(CONV_W):
        off = CONV_HDR - (CONV_W - 1) + tap
        acc = acc + w_ref[tap:tap + 1, cols] * cs_ref[off:off + rows, cols]
    return _silu(acc)


def _gate_forms(gcol):
    grow = gcol.T[0:8, :]
    lane = lax.broadcasted_iota(jnp.int32, grow.shape, 1)
    brow = grow
    sh = 1
    while sh < CHUNK:
        brow = brow + jnp.where(lane >= sh, pltpu.roll(brow, sh, 1), 0.0)
        sh *= 2
    pad = jnp.zeros((LANES - 8, CHUNK), F32)
    bcol = jnp.concatenate([brow, pad], axis=0).T
    cbm = bcol - pltpu.roll(gcol, ML_HEADS, 1)
    return grow, brow, cbm


def _mlstm_chunk(q, k, vt, li_row, b_row, cb_col, c_ext, m_prev, tri):
    n = q.shape[0]
    dt = jnp.where(tri, b_row - cb_col, NEG_BIG)
    inter = b_row + m_prev
    m_row = jnp.maximum(inter, jnp.max(dt, axis=0, keepdims=True))
    w_inter = jnp.exp(inter - m_row)
    sct = _dot_nt(k, q) * jnp.exp(dt - m_row)
    het = (w_inter * _dot_nt(c_ext.astype(BF16), q)
           + _dot(vt.astype(BF16), sct.astype(BF16)))
    den = het[ML_DV:ML_DV + 1, :]
    ht = het[0:ML_DV, :] / jnp.maximum(jnp.abs(den), jnp.exp(-m_row))
    b_last = b_row[:, n - 1:n]
    a_row = b_last - b_row + li_row
    m_new = jnp.maximum(b_last + m_prev, jnp.max(a_row, axis=1, keepdims=True))
    w_c = jnp.exp(b_last + m_prev - m_new)
    w_a = jnp.exp(a_row - m_new)
    c_new = w_c * c_ext + _dot((vt * w_a).astype(BF16), k)
    return ht, c_new, m_new


def _mlstm_chunk_all_heads(c, g_ref, q_ref, k_ref, vt_ref, c_ref, m_ref, hs_ref):
    r0 = c * CHUNK if isinstance(c, int) else pl.multiple_of(c * CHUNK, CHUNK)
    gcol = g_ref[pl.ds(r0, CHUNK), :]
    grow, brow, cbm = _gate_forms(gcol)
    row = lax.broadcasted_iota(jnp.int32, (CHUNK, CHUNK), 0)
    col = lax.broadcasted_iota(jnp.int32, (CHUNK, CHUNK), 1)
    tri = row <= col
    for h in range(ML_HEADS):
        hd = slice(ML_DK * h, ML_DK * (h + 1))
        ht, c_new, m_new = _mlstm_chunk(
            q_ref[pl.ds(r0, CHUNK), hd], k_ref[pl.ds(r0, CHUNK), hd], vt_ref[h, c],
            grow[h:h + 1, :], brow[ML_HEADS + h:ML_HEADS + h + 1, :],
            cbm[:, ML_HEADS + h:ML_HEADS + h + 1],
            c_ref[h], m_ref[h:h + 1, 0:1], tri)
        if hs_ref is not None:
            hs_ref[pl.ds(r0, CHUNK), ML_DV * h:ML_DV * (h + 1)] = ht.T
        c_ref[h] = c_new
        m_ref[h:h + 1, :] = jnp.broadcast_to(m_new, (1, LANES))


def _gates(gi):
    lane = lax.broadcasted_iota(jnp.int32, gi.shape, 1)
    return jnp.where(lane < ML_HEADS, gi, jnp.where(lane < 2 * ML_HEADS, _log_sigmoid(gi), 0.0))


def _store_values_ext(vt_all, vt_ref):
    nchunk = vt_all.shape[1] // CHUNK
    sub = lax.broadcasted_iota(jnp.int32, (ML_EXT - ML_DV, CHUNK), 0)
    tail = jnp.where(sub == 0, 1.0, 0.0)
    for h in range(ML_HEADS):
        for c in range(nchunk):
            vt_ref[h, c, 0:ML_DV, :] = vt_all[ML_DV * h:ML_DV * (h + 1), CHUNK * c:CHUNK * (c + 1)]
            vt_ref[h, c, ML_DV:ML_EXT, :] = tail


def _meta_kernel(xm_ref, gpre_ref, wk_ref, wv_ref, wmqk_ref, wmvt_ref, wif_ref,
                 convw_ref, convb_ref, gbias_ref, rc_ref, rs1_ref, rs2_ref,
                 kmeta_ref, vmeta_ref, tail_ref, c0_ref, m0_ref,
                 cs_s, q_s, k_s, vt_s, g_s):
    rows = xm_ref.shape[0]
    u = _rmsnorm(xm_ref[...], gpre_ref[...]).astype(BF16)
    valid = lax.broadcasted_iota(jnp.int32, (rows, 1), 0) >= rows - N_META
    valid_t = lax.broadcasted_iota(jnp.int32, (1, rows), 1) >= rows - N_META

    kc = _dot(u, wk_ref[...])
    for cb in range(kc.shape[1] // LANES):
        sl = slice(LANES * cb, LANES * (cb + 1))
        kmeta_ref[:, sl] = _rope(kc[:, sl], rc_ref[...], rs1_ref[...], rs2_ref[...])
    vmeta_ref[...] = _dot(u, wv_ref[...])

    cs_s[0:CONV_HDR, :] = jnp.zeros((CONV_HDR, D_MODEL), F32)
    cs_s[CONV_HDR:, :] = _dot(u, wmqk_ref[...])
    tail_ref[...] = cs_s[rows:rows + CONV_HDR, :]
    half = D_MODEL // 2
    for cc in range(4):
        cols = slice(256 * cc, 256 * (cc + 1))
        act = jnp.where(valid, _causal_conv_silu(cs_s, convw_ref, convb_ref, rows, cols), 0.0)
        if cc < 2:
            q_s[:, cols] = act.astype(BF16)
        else:
            k_s[:, 256 * cc - half:256 * (cc + 1) - half] = (act * (ML_DK ** -0.5)).astype(BF16)

    _store_values_ext(jnp.where(valid_t, _dot_nt(wmvt_ref[...], u), 0.0), vt_s)

    gi = _dot(u, wif_ref[...]) + gbias_ref[...]
    lane = lax.broadcasted_iota(jnp.int32, gi.shape, 1)
    pad_gate = jnp.where(lane < ML_HEADS, NEG_BIG, 0.0)
    g_s[...] = jnp.where(valid, _gates(gi), pad_gate)

    c0_ref[...] = jnp.zeros(c0_ref.shape, F32)
    m0_ref[...] = jnp.zeros(m0_ref.shape, F32)
    _mlstm_chunk_all_heads(0, g_s, q_s, k_s, vt_s, c0_ref, m0_ref, None)


def _attn_kernel(x_ref, gpre_ref, wq_ref, wk_ref, wvt_ref, wz_ref, wga_ref, wao_ref,
                 rc_ref, rs1_ref, rs2_ref, km2_ref, vmt_ref, bt_ref, bt0_ref, sink_ref,
                 out_ref,
                 u_s, q_s, kk_s, vt_s, a_s, gated_s):
    j = pl.program_id(1)
    tile = x_ref.shape[1]
    nblk = tile // ATTN_BLOCK
    pairs = N_HEADS // 2

    u_s[...] = _rmsnorm(x_ref[0], gpre_ref[...]).astype(BF16)

    c_t, s1, s2 = rc_ref[...], rs1_ref[...], rs2_ref[...]
    low = lax.broadcasted_iota(jnp.int32, (tile, LANES), 1) < HEAD_DIM
    for half in range(2):
        qc = _dot(u_s[...], wq_ref[:, 512 * half:512 * (half + 1)])
        for cc in range(4):
            blk = _rope(qc[:, LANES * cc:LANES * (cc + 1)], c_t, s1, s2) * (HEAD_DIM ** -0.5)
            q_s[4 * half + cc, 0] = jnp.where(low, blk, 0.0).astype(BF16)
            q_s[4 * half + cc, 1] = jnp.where(low, 0.0, blk).astype(BF16)

    @pl.when(j == 0)
    def _():
        kk_s[:, 0:ATTN_BLOCK, :] = jnp.zeros((N_KV, ATTN_BLOCK, LANES), BF16)
        vt_s[:, 0] = jnp.zeros((N_KV, HEAD_DIM, ATTN_BLOCK), BF16)

    kc = _dot(u_s[...], wk_ref[...])
    for cb in range(N_KV // 2):
        blk = _rope(kc[:, LANES * cb:LANES * (cb + 1)], c_t, s1, s2)
        swapped = pltpu.roll(blk, HEAD_DIM, 1)
        kk_s[2 * cb, ATTN_BLOCK:, :] = jnp.where(low, blk, swapped).astype(BF16)
        kk_s[2 * cb + 1, ATTN_BLOCK:, :] = jnp.where(low, swapped, blk).astype(BF16)
    vt_all = _dot_nt(wvt_ref[...], u_s[...])
    for g in range(N_KV):
        for bi in range(nblk):
            vt_s[g, 1 + bi] = vt_all[HEAD_DIM * g:HEAD_DIM * (g + 1),
                                     ATTN_BLOCK * bi:ATTN_BLOCK * (bi + 1)].astype(BF16)

    pad_rows = jnp.zeros((LANES - N_META, 2 * ATTN_BLOCK), BF16)

    def block_body(i, carry):
        r0 = pl.multiple_of(i * ATTN_BLOCK, ATTN_BLOCK)
        first = jnp.logical_and(j == 0, i == 0)
        bias = jnp.where(first, bt0_ref[...], bt_ref[...])
        for c in range(pairs):
            g = c // 2
            qst = jnp.concatenate([q_s[c, 0, pl.ds(r0, ATTN_BLOCK), :],
                                   q_s[c, 1, pl.ds(r0, ATTN_BLOCK), :]], axis=0)
            band = jnp.concatenate([km2_ref[g], kk_s[g, pl.ds(r0, 2 * ATTN_BLOCK), :]], axis=0)
            st = _dot_nt(band, qst) + bias
            sink = sink_ref[c, 0:1, :]
            mx = jnp.maximum(jnp.max(st, axis=0, keepdims=True), sink)
            p = jnp.exp(st - mx)
            den = jnp.sum(p, axis=0, keepdims=True) + jnp.exp(sink - mx)
            pb = p.astype(BF16)
            vt = jnp.concatenate([vt_s[g, i], vt_s[g, i + 1]], axis=1)
            pmeta = jnp.concatenate([pb[0:N_META], pad_rows], axis=0)
            ot = (_dot(vt, pb[N_META:]) + _dot(vmt_ref[g], pmeta)) * (1.0 / den)
            o = jnp.concatenate([ot[:, :ATTN_BLOCK], ot[:, ATTN_BLOCK:]], axis=0).T
            a_s[c, pl.ds(r0, ATTN_BLOCK), :] = o
        return carry

    lax.fori_loop(0, nblk, block_body, 0)

    kk_s[:, 0:ATTN_BLOCK, :] = kk_s[:, tile:tile + ATTN_BLOCK, :]
    vt_s[:, 0] = vt_s[:, nblk]

    for g4 in range(4):
        cols = slice(256 * g4, 256 * (g4 + 1))
        att = jnp.concatenate([a_s[2 * g4], a_s[2 * g4 + 1]], axis=1)
        gated_s[:, cols] = (att * _silu(_dot(u_s[...], wz_ref[:, cols]))).astype(BF16)
    for half in range(2):
        cols = slice(512 * half, 512 * (half + 1))
        y = _dot(gated_s[...], wao_ref[:, cols])
        ga = _dot(u_s[...], wga_ref[:, cols])
        out_ref[0, :, cols] = (_sigmoid(ga) * y).astype(out_ref.dtype)


def _mlstm_kernel(x_ref, ya_ref, gpre_ref, gpost_ref, wmqk_ref, wmvt_ref, wif_ref, wmo_ref, wmz_ref,
                  wgm_ref, wmlo_ref, wout_ref, convw_ref, convb_ref, gbias_ref, hnorm_ref,
                  tail0_ref, c0_ref, m0_ref,
                  out_ref,
                  u_s, cs_s, q_s, k_s, vt_s, g_s, hs_s, hm_s, mg_s, c_s, m_s):
    j = pl.program_id(1)
    tile = x_ref.shape[1]

    u_s[...] = _rmsnorm(x_ref[0], gpre_ref[...]).astype(BF16)

    @pl.when(j == 0)
    def _():
        cs_s[0:CONV_HDR, :] = tail0_ref[...]
        c_s[...] = c0_ref[...]
        m_s[...] = m0_ref[...]

    for half in range(2):
        cols = slice(512 * half, 512 * (half + 1))
        cs_s[CONV_HDR:, cols] = _dot(u_s[...], wmqk_ref[:, cols])
    hw = D_MODEL // 2
    for cc in range(4):
        cols = slice(256 * cc, 256 * (cc + 1))
        act = _causal_conv_silu(cs_s, convw_ref, convb_ref, tile, cols)
        if cc < 2:
            q_s[:, cols] = act.astype(BF16)
        else:
            k_s[:, 256 * cc - hw:256 * (cc + 1) - hw] = (act * (ML_DK ** -0.5)).astype(BF16)
    cs_s[0:CONV_HDR, :] = cs_s[tile:tile + CONV_HDR, :]

    _store_values_ext(_dot_nt(wmvt_ref[...], u_s[...]), vt_s)
    g_s[...] = _gates(_dot(u_s[...], wif_ref[...]) + gbias_ref[...])

    def chunk_body(c, carry):
        _mlstm_chunk_all_heads(c, g_s, q_s, k_s, vt_s, c_s, m_s, hs_s)
        return carry

    lax.fori_loop(0, tile // CHUNK, chunk_body, 0)

    for h in range(ML_HEADS):
        cols = slice(ML_DV * h, ML_DV * (h + 1))
        hg = _sigmoid(_dot(u_s[...], wmo_ref[:, cols])) * hs_s[:, cols]
        hc = hg - jnp.mean(hg, axis=-1, keepdims=True)
        y = hc * lax.rsqrt(jnp.mean(hc * hc, axis=-1, keepdims=True) + LN_EPS) * hnorm_ref[:, cols]
        hm_s[:, cols] = (y * _silu(_dot(u_s[...], wmz_ref[:, cols]))).astype(BF16)
    for half in range(2):
        cols = slice(512 * half, 512 * (half + 1))
        ym = _dot(hm_s[...], wmlo_ref[:, cols])
        gm = _sigmoid(_dot(u_s[...], wgm_ref[:, cols]))
        mg_s[:, cols] = (ya_ref[0, :, cols].astype(F32) + gm * ym).astype(BF16)
    o = _dot(mg_s[...], wout_ref[...])
    out_ref[0] = x_ref[0] + _rmsnorm(o, gpost_ref[...])


def _rope_tables(pos):
    half = ROPE_DIM // 2
    inv_freq = ROPE_THETA ** (-jnp.arange(0, ROPE_DIM, 2, dtype=F32) / ROPE_DIM)
    ang = pos[:, None] * inv_freq[None, :]
    cos, sin = jnp.cos(ang), jnp.sin(ang)
    ones = jnp.ones((pos.shape[0], HEAD_DIM - ROPE_DIM), F32)
    zeros_h = jnp.zeros((pos.shape[0], half), F32)
    zeros_r = jnp.zeros_like(ones)
    c_head = jnp.concatenate([cos, cos, ones], axis=1)
    s1_head = jnp.concatenate([-sin, zeros_h, zeros_r], axis=1)
    s2_head = jnp.concatenate([zeros_h, sin, zeros_r], axis=1)
    rep = LANES // HEAD_DIM
    return (jnp.tile(c_head, (1, rep)), jnp.tile(s1_head, (1, rep)), jnp.tile(s2_head, (1, rep)))


def _band_bias():
    kj = np.arange(2 * ATTN_BLOCK)[:, None]
    qi = np.arange(ATTN_BLOCK)[None, :]
    vis = (kj > qi) & (kj <= qi + ATTN_BLOCK)
    meta = np.zeros((N_META, ATTN_BLOCK), np.float32)
    bt = np.concatenate([meta, np.where(vis, 0.0, NEG_BIG)], axis=0).astype(np.float32)
    bt0 = np.concatenate([meta, np.where(vis & (kj >= ATTN_BLOCK), 0.0, NEG_BIG)], axis=0).astype(np.float32)
    return jnp.asarray(np.tile(bt, (1, 2))), jnp.asarray(np.tile(bt0, (1, 2)))


def _full(shape, single=True):
    nd = len(shape)
    kw = {"pipeline_mode": pl.Buffered(1)} if single else {}
    return pl.BlockSpec(shape, lambda *_: (0,) * nd, **kw)


def kernel(x, meta_tokens, norm_pre, w_in, attn_sinks, conv_w, conv_b, mlstm_gate_bias, mlstm_head_norm,
           w_attn_out, w_mlstm_out, w_out, norm_post):
    B, S, D = x.shape
    assert D == D_MODEL and w_in.shape[0] == 1 and meta_tokens.shape == (N_META, D_MODEL)
    tile = min(SEQ_TILE, S)
    assert S % tile == 0 and tile % CHUNK == 0 and tile % ATTN_BLOCK == 0
    nt = S // tile
    nblk = tile // ATTN_BLOCK
    nchunk = tile // CHUNK

    w = w_in[0].astype(BF16)
    o = 0
    parts = {}
    for name, width in (("q", 1024), ("k", 256), ("v", 256), ("z", 1024), ("mqk", 1024), ("mv", 1024),
                        ("mi", 4), ("mf", 4), ("mo", 1024), ("mz", 1024), ("ga", 1024), ("gm", 1024)):
        parts[name] = w[:, o:o + width]
        o += width
    wvt = parts["v"].T
    wmvt = parts["mv"].T
    wif = jnp.concatenate([parts["mi"], parts["mf"],
                           jnp.zeros((D, LANES - 2 * ML_HEADS), BF16)], axis=1)
    gbias = jnp.concatenate([mlstm_gate_bias[0].astype(F32),
                             jnp.zeros((LANES - 2 * ML_HEADS,), F32)])[None, :]
    gpre = norm_pre[0].astype(F32)[None, :]
    gpost = norm_post[0].astype(F32)[None, :]
    convw = conv_w[0].astype(F32)
    convb = conv_b[0].astype(F32)[None, :]
    hnorm = mlstm_head_norm[0].astype(F32)[None, :]
    wao = w_attn_out[0].astype(BF16)
    wmlo = w_mlstm_out[0].astype(BF16)
    wout = w_out[0].astype(BF16)
    sinks = attn_sinks[0].astype(F32)
    sink_rows = jnp.broadcast_to(jnp.repeat(sinks.reshape(N_HEADS // 2, 2), ATTN_BLOCK, axis=1)[:, None, :],
                                 (N_HEADS // 2, 8, 2 * ATTN_BLOCK))

    xm = jnp.concatenate([jnp.zeros((CHUNK - N_META, D), F32), meta_tokens.astype(F32)], axis=0)
    mpos = jnp.maximum(jnp.arange(CHUNK, dtype=F32) - (CHUNK - N_META), 0.0)
    mrc, mrs1, mrs2 = _rope_tables(mpos)
    kv_w = N_KV * HEAD_DIM
    meta_out = pl.pallas_call(
        _meta_kernel,
        out_shape=(jax.ShapeDtypeStruct((CHUNK, kv_w), F32), jax.ShapeDtypeStruct((CHUNK, kv_w), F32),
                   jax.ShapeDtypeStruct((CONV_HDR, D), F32),
                   jax.ShapeDtypeStruct((ML_HEADS, ML_EXT, ML_DK), F32),
                   jax.ShapeDtypeStruct((8, LANES), F32)),
        scratch_shapes=[pltpu.VMEM((CONV_HDR + CHUNK, D), F32), pltpu.VMEM((CHUNK, D // 2), BF16),
                        pltpu.VMEM((CHUNK, D // 2), BF16), pltpu.VMEM((ML_HEADS, 1, ML_EXT, CHUNK), F32),
                        pltpu.VMEM((CHUNK, LANES), F32)],
        compiler_params=pltpu.CompilerParams(vmem_limit_bytes=VMEM_LIMIT),
        name="meta_tokens",
    )(xm, gpre, parts["k"], parts["v"], parts["mqk"], wmvt, wif, convw, convb, gbias,
      mrc, mrs1, mrs2)
    kmeta, vmeta, tail0, c0, m0 = meta_out
    km = kmeta[CHUNK - N_META:].reshape(N_META, N_KV, HEAD_DIM).transpose(1, 0, 2)
    km2 = jnp.concatenate([km, km], axis=2).astype(BF16)
    vm = vmeta[CHUNK - N_META:].reshape(N_META, N_KV, HEAD_DIM).transpose(1, 2, 0)
    vmt = jnp.concatenate([vm, jnp.zeros((N_KV, HEAD_DIM, LANES - N_META), F32)], axis=2).astype(BF16)

    rc, rs1, rs2 = _rope_tables(jnp.arange(S, dtype=F32) + N_META)
    bt, bt0 = _band_bias()
    x_spec = pl.BlockSpec((1, tile, D), lambda b, j: (b, j, 0))
    rope_spec = pl.BlockSpec((tile, LANES), lambda b, j: (j, 0))
    seq_params = pltpu.CompilerParams(dimension_semantics=("arbitrary", "arbitrary"),
                                      vmem_limit_bytes=VMEM_LIMIT)
    ya = pl.pallas_call(
        _attn_kernel,
        grid=(B, nt),
        in_specs=[x_spec, _full((1, D)),
                  _full((D, 1024)), _full((D, kv_w)), _full((kv_w, D)), _full((D, 1024)),
                  _full((D, 1024)), _full((D, D)),
                  rope_spec, rope_spec, rope_spec,
                  _full((N_KV, N_META, LANES)), _full((N_KV, HEAD_DIM, LANES)),
                  _full((BAND, 2 * ATTN_BLOCK)), _full((BAND, 2 * ATTN_BLOCK)),
                  _full((N_HEADS // 2, 8, 2 * ATTN_BLOCK))],
        out_specs=pl.BlockSpec((1, tile, D), lambda b, j: (b, j, 0)),
        out_shape=jax.ShapeDtypeStruct((B, S, D), BF16),
        scratch_shapes=[pltpu.VMEM((tile, D), BF16),
                        pltpu.VMEM((N_HEADS // 2, 2, tile, LANES), BF16),
                        pltpu.VMEM((N_KV, ATTN_BLOCK + tile, LANES), BF16),
                        pltpu.VMEM((N_KV, nblk + 1, HEAD_DIM, ATTN_BLOCK), BF16),
                        pltpu.VMEM((N_HEADS // 2, tile, LANES), F32),
                        pltpu.VMEM((tile, D), BF16)],
        compiler_params=seq_params,
        name="attn_branch",
    )(x, gpre, parts["q"], parts["k"], wvt, parts["z"], parts["ga"], wao,
      rc, rs1, rs2, km2, vmt, bt, bt0, sink_rows)

    out = pl.pallas_call(
        _mlstm_kernel,
        grid=(B, nt),
        in_specs=[x_spec, pl.BlockSpec((1, tile, D), lambda b, j: (b, j, 0)),
                  _full((1, D)), _full((1, D)),
                  _full((D, 1024)), _full((1024, D)), _full((D, LANES)), _full((D, 1024)),
                  _full((D, 1024)), _full((D, 1024)), _full((D, D)), _full((D, D)),
                  _full((CONV_W, D)), _full((1, D)), _full((1, LANES)), _full((1, D)),
                  _full((CONV_HDR, D)), _full((ML_HEADS, ML_EXT, ML_DK)), _full((8, LANES))],
        out_specs=pl.BlockSpec((1, tile, D), lambda b, j: (b, j, 0)),
        out_shape=jax.ShapeDtypeStruct((B, S, D), x.dtype),
        scratch_shapes=[pltpu.VMEM((tile, D), BF16),
                        pltpu.VMEM((CONV_HDR + tile, D), F32),
                        pltpu.VMEM((tile, D // 2), BF16),
                        pltpu.VMEM((tile, D // 2), BF16),
                        pltpu.VMEM((ML_HEADS, nchunk, ML_EXT, CHUNK), F32),
                        pltpu.VMEM((tile, LANES), F32),
                        pltpu.VMEM((tile, D), F32),
                        pltpu.VMEM((tile, D), BF16),
                        pltpu.VMEM((tile, D), BF16),
                        pltpu.VMEM((ML_HEADS, ML_EXT, ML_DK), F32),
                        pltpu.VMEM((8, LANES), F32)],
        compiler_params=seq_params,
        name="mlstm_merge_out",
    )(x, ya, gpre, gpost, parts["mqk"], wmvt, wif, parts["mo"], parts["mz"], parts["gm"],
      wmlo, wout, convw, convb, gbias, hnorm, tail0, c0, m0)
    return out
```

```python
import numpy as np
import jax
import jax.numpy as jnp
from jax import lax
from jax.experimental import pallas as pl
from jax.experimental.pallas import tpu as pltpu

F32 = jnp.float32
BF16 = jnp.bfloat16

D_MODEL = 1024
N_META = 16
HEAD_DIM = 64
N_HEADS = 16
N_KV = 4
ATTN_BLOCK = 128
BAND = N_META + 2 * ATTN_BLOCK
ROPE_DIM = 16
ROPE_THETA = 500000.0
ML_HEADS = 4
ML_DV = 256
ML_DK = 128
ML_EXT = ML_DV + 16
CONV_W = 4
CHUNK = 128
RMS_EPS = 1e-6
LN_EPS = 1e-6
NEG_BIG = -1e30
LANES = 128
CONV_HDR = 8
SEQ_TILE = 512
ROW_GROUP = 256
VMEM_LIMIT = 56 * 1024 * 1024


def _dot(a, b):
    return jnp.dot(a, b, preferred_element_type=F32)


def _dot_nt(a, b):
    return lax.dot_general(a, b, (((1,), (1,)), ((), ())), preferred_element_type=F32)


def _sigmoid(x):
    return 0.5 * jnp.tanh(0.5 * x) + 0.5


def _silu(x):
    return x * _sigmoid(x)


def _log_sigmoid(x):
    return jnp.minimum(x, 0.0) - jnp.log(1.0 + jnp.exp(-jnp.abs(x)))


def _rmsnorm(xf, g):
    ms = jnp.mean(xf * xf, axis=-1, keepdims=True)
    return xf * lax.rsqrt(ms + RMS_EPS) * g


def _rope(blk, c_t, s1, s2):
    return blk * c_t + pltpu.roll(blk, LANES - 8, 1) * s1 + pltpu.roll(blk, 8, 1) * s2


def _causal_conv_silu(cs_ref, w_ref, b_ref, row0, rows, cols):
    acc = b_ref[:, cols]
    for tap in range(CONV_W):
        off = CONV_HDR - (CONV_W - 1) + tap + row0
        acc = acc + w_ref[tap:tap + 1, cols] * cs_ref[off:off + rows, cols]
    return _silu(acc)


def _conv_qk(cs_ref, w_ref, b_ref, q_ref, k_ref, row0, rows, valid=None):
    half = D_MODEL // 2
    for cc in range(4):
        cols = slice(256 * cc, 256 * (cc + 1))
        act = _causal_conv_silu(cs_ref, w_ref, b_ref, row0, rows, cols)
        if valid is not None:
            act = jnp.where(valid, act, 0.0)
        if cc < 2:
            q_ref[row0:row0 + rows, cols] = act.astype(BF16)
        else:
            k_ref[row0:row0 + rows, 256 * cc - half:256 * (cc + 1) - half] = (
                act * (ML_DK ** -0.5)).astype(BF16)


def _gate_forms(gcol):
    grow = gcol.T[0:8, :]
    lane = lax.broadcasted_iota(jnp.int32, grow.shape, 1)
    brow = grow
    sh = 1
    while sh < CHUNK:
        brow = brow + jnp.where(lane >= sh, pltpu.roll(brow, sh, 1), 0.0)
        sh *= 2
    pad = jnp.zeros((LANES - 8, CHUNK), F32)
    bcol = jnp.concatenate([brow, pad], axis=0).T
    cbm = bcol - pltpu.roll(gcol, ML_HEADS, 1)
    return grow, brow, cbm


def _store_gate_forms(g, gr_ref, br_ref, cb_ref):
    for c in range(g.shape[0] // CHUNK):
        gr_ref[c], br_ref[c], cb_ref[c] = _gate_forms(g[CHUNK * c:CHUNK * (c + 1), :])


def _mlstm_chunk_all_heads(c, gr_ref, br_ref, cb_ref, q_ref, k_ref, vt_ref, c_ref, m_ref, hs_ref,
                           between=None, after=None):
    r0 = c * CHUNK if isinstance(c, int) else pl.multiple_of(c * CHUNK, CHUNK)
    n = CHUNK
    grow, brow, cbm = gr_ref[c], br_ref[c], cb_ref[c]
    row = lax.broadcasted_iota(jnp.int32, (n, n), 0)
    col = lax.broadcasted_iota(jnp.int32, (n, n), 1)
    tri = row <= col
    staged = []
    for h in range(ML_HEADS):
        hd = slice(ML_DK * h, ML_DK * (h + 1))
        q, k, vt = q_ref[pl.ds(r0, n), hd], k_ref[pl.ds(r0, n), hd], vt_ref[h, c]
        li_row = grow[h:h + 1, :]
        b_row = brow[ML_HEADS + h:ML_HEADS + h + 1, :]
        cb_col = cbm[:, ML_HEADS + h:ML_HEADS + h + 1]
        c_ext, m_prev = c_ref[h], m_ref[h:h + 1, 0:1]
        s_raw = _dot_nt(k, q)
        cq = _dot_nt(c_ext.astype(BF16), q)
        b_last = b_row[:, n - 1:n]
        a_row = b_last - b_row + li_row
        m_new = jnp.maximum(b_last + m_prev, jnp.max(a_row, axis=1, keepdims=True))
        w_c = jnp.exp(b_last + m_prev - m_new)
        w_a = jnp.exp(a_row - m_new)
        c_ref[h] = w_c * c_ext + _dot((vt * w_a).astype(BF16), k)
        m_ref[h:h + 1, :] = jnp.broadcast_to(m_new, (1, LANES))
        staged.append((s_raw, cq, vt, b_row, cb_col, m_prev))
    if between is not None:
        between()
    if hs_ref is None:
        return
    for h, (s_raw, cq, vt, b_row, cb_col, m_prev) in enumerate(staged):
        dt = jnp.where(tri, b_row - cb_col, NEG_BIG)
        inter = b_row + m_prev
        m_row = jnp.maximum(inter, jnp.max(dt, axis=0, keepdims=True))
        sct = s_raw * jnp.exp(dt - m_row)
        het = jnp.exp(inter - m_row) * cq + _dot(vt.astype(BF16), sct.astype(BF16))
        den = het[ML_DV:ML_DV + 1, :]
        ht = het[0:ML_DV, :] / jnp.maximum(jnp.abs(den), jnp.exp(-m_row))
        hs_ref[pl.ds(r0, n), ML_DV * h:ML_DV * (h + 1)] = ht.T
    if after is not None:
        after()


def _gates(gi):
    lane = lax.broadcasted_iota(jnp.int32, gi.shape, 1)
    return jnp.where(lane < ML_HEADS, gi, jnp.where(lane < 2 * ML_HEADS, _log_sigmoid(gi), 0.0))


def _store_values_ext(vt_all, vt_ref):
    nchunk = vt_all.shape[1] // CHUNK
    sub = lax.broadcasted_iota(jnp.int32, (ML_EXT - ML_DV, CHUNK), 0)
    tail = jnp.where(sub == 0, 1.0, 0.0)
    for h in range(ML_HEADS):
        for c in range(nchunk):
            vt_ref[h, c, 0:ML_DV, :] = vt_all[ML_DV * h:ML_DV * (h + 1), CHUNK * c:CHUNK * (c + 1)]
            vt_ref[h, c, ML_DV:ML_EXT, :] = tail


def _meta_kernel(xm_ref, gpre_ref, wk_ref, wv_ref, wmqk_ref, wmvt_ref, wif_ref,
                 convw_ref, convb_ref, gbias_ref, rc_ref, rs1_ref, rs2_ref,
                 kmeta_ref, vmeta_ref, tail_ref, c0_ref, m0_ref,
                 cs_s, q_s, k_s, vt_s, gr_s, br_s, cb_s):
    rows = xm_ref.shape[0]
    u = _rmsnorm(xm_ref[...], gpre_ref[...]).astype(BF16)
    valid = lax.broadcasted_iota(jnp.int32, (rows, 1), 0) >= rows - N_META
    valid_t = lax.broadcasted_iota(jnp.int32, (1, rows), 1) >= rows - N_META

    kc = _dot(u, wk_ref[...])
    for cb in range(kc.shape[1] // LANES):
        sl = slice(LANES * cb, LANES * (cb + 1))
        kmeta_ref[:, sl] = _rope(kc[:, sl], rc_ref[...], rs1_ref[...], rs2_ref[...])
    vmeta_ref[...] = _dot(u, wv_ref[...])

    cs_s[0:CONV_HDR, :] = jnp.zeros((CONV_HDR, D_MODEL), F32)
    cs_s[CONV_HDR:, :] = _dot(u, wmqk_ref[...])
    tail_ref[...] = cs_s[rows:rows + CONV_HDR, :]
    _conv_qk(cs_s, convw_ref, convb_ref, q_s, k_s, 0, rows, valid)

    _store_values_ext(jnp.where(valid_t, _dot_nt(wmvt_ref[...], u), 0.0), vt_s)

    gi = _dot(u, wif_ref[...]) + gbias_ref[...]
    lane = lax.broadcasted_iota(jnp.int32, gi.shape, 1)
    pad_gate = jnp.where(lane < ML_HEADS, NEG_BIG, 0.0)
    _store_gate_forms(jnp.where(valid, _gates(gi), pad_gate), gr_s, br_s, cb_s)

    c0_ref[...] = jnp.zeros(c0_ref.shape, F32)
    m0_ref[...] = jnp.zeros(m0_ref.shape, F32)
    _mlstm_chunk_all_heads(0, gr_s, br_s, cb_s, q_s, k_s, vt_s, c0_ref, m0_ref, None)


def _attn_kernel(x_ref, gpre_ref, wq_ref, wk_ref, wvt_ref, wz_ref, wga_ref, wao_ref,
                 rc_ref, rs1_ref, rs2_ref, km2_ref, vmt_ref, bt_ref, bt0_ref, sink_ref,
                 out_ref,
                 u_s, q_s, kk_s, vt_s, a_s, gated_s, zg_s, ga_s):
    j = pl.program_id(1)
    tile = x_ref.shape[1]
    nblk = tile // ATTN_BLOCK
    pairs = N_HEADS // 2

    u_s[...] = _rmsnorm(x_ref[0], gpre_ref[...]).astype(BF16)

    c_t, s1, s2 = rc_ref[...], rs1_ref[...], rs2_ref[...]
    low = lax.broadcasted_iota(jnp.int32, (tile, LANES), 1) < HEAD_DIM
    for half in range(2):
        qc = _dot(u_s[...], wq_ref[:, 512 * half:512 * (half + 1)])
        for cc in range(4):
            blk = _rope(qc[:, LANES * cc:LANES * (cc + 1)], c_t, s1, s2) * (HEAD_DIM ** -0.5)
            q_s[4 * half + cc, 0] = jnp.where(low, blk, 0.0).astype(BF16)
            q_s[4 * half + cc, 1] = jnp.where(low, 0.0, blk).astype(BF16)

    @pl.when(j == 0)
    def _():
        kk_s[:, 0:ATTN_BLOCK, :] = jnp.zeros((N_KV, ATTN_BLOCK, LANES), BF16)
        vt_s[:, 0] = jnp.zeros((N_KV, HEAD_DIM, ATTN_BLOCK), BF16)

    kc = _dot(u_s[...], wk_ref[...])
    for cb in range(N_KV // 2):
        blk = _rope(kc[:, LANES * cb:LANES * (cb + 1)], c_t, s1, s2)
        swapped = pltpu.roll(blk, HEAD_DIM, 1)
        kk_s[2 * cb, ATTN_BLOCK:, :] = jnp.where(low, blk, swapped).astype(BF16)
        kk_s[2 * cb + 1, ATTN_BLOCK:, :] = jnp.where(low, swapped, blk).astype(BF16)
    vt_all = _dot_nt(wvt_ref[...], u_s[...])
    for g in range(N_KV):
        for bi in range(nblk):
            vt_s[g, 1 + bi] = vt_all[HEAD_DIM * g:HEAD_DIM * (g + 1),
                                     ATTN_BLOCK * bi:ATTN_BLOCK * (bi + 1)].astype(BF16)

    pad_rows = jnp.zeros((LANES - N_META, 2 * ATTN_BLOCK), BF16)

    for i in range(nblk):
        r0 = i * ATTN_BLOCK
        bias = jnp.where(j == 0, bt0_ref[...], bt_ref[...]) if i == 0 else bt_ref[...]
        scores = []
        for c in range(pairs):
            g = c // 2
            qst = jnp.concatenate([q_s[c, 0, pl.ds(r0, ATTN_BLOCK), :],
                                   q_s[c, 1, pl.ds(r0, ATTN_BLOCK), :]], axis=0)
            band = jnp.concatenate([km2_ref[g], kk_s[g, pl.ds(r0, 2 * ATTN_BLOCK), :]], axis=0)
            scores.append(_dot_nt(band, qst))
        zga = _dot(u_s[...], jnp.concatenate([wz_ref[i], wga_ref[i]], axis=1))
        zg_s[i] = _silu(zga[:, :ML_DV])
        ga_s[i] = _sigmoid(zga[:, ML_DV:])
        probs = []
        for c in range(pairs):
            st = scores[c] + bias
            sink = sink_ref[c, 0:1, :]
            mx = jnp.maximum(jnp.max(st, axis=0, keepdims=True), sink)
            p = jnp.exp(st - mx)
            den = jnp.sum(p, axis=0, keepdims=True) + jnp.exp(sink - mx)
            probs.append((p.astype(BF16), 1.0 / den))
        for c in range(pairs):
            g = c // 2
            pb, rinv = probs[c]
            vt = jnp.concatenate([vt_s[g, i], vt_s[g, i + 1]], axis=1)
            pmeta = jnp.concatenate([pb[0:N_META], pad_rows], axis=0)
            ot = (_dot(vt, pb[N_META:]) + _dot(vmt_ref[g], pmeta)) * rinv
            o = jnp.concatenate([ot[:, :ATTN_BLOCK], ot[:, ATTN_BLOCK:]], axis=0).T
            a_s[c, pl.ds(r0, ATTN_BLOCK), :] = o

    kk_s[:, 0:ATTN_BLOCK, :] = kk_s[:, tile:tile + ATTN_BLOCK, :]
    vt_s[:, 0] = vt_s[:, nblk]

    width = zg_s.shape[2]
    per = width // LANES
    for g4 in range(nblk):
        att = jnp.concatenate([a_s[per * g4 + p] for p in range(per)], axis=1)
        gated_s[:, width * g4:width * (g4 + 1)] = (att * zg_s[g4]).astype(BF16)
    for g4 in range(nblk):
        cols = slice(width * g4, width * (g4 + 1))
        out_ref[0, :, cols] = (ga_s[g4] * _dot(gated_s[...], wao_ref[:, cols])).astype(out_ref.dtype)


def _mlstm_kernel(x_ref, ya_ref, gpre_ref, gpost_ref, wmqk_ref, wmvt_ref, wif_ref, wmo_ref, wmz_ref,
                  wgm_ref, wmlo_ref, wout_ref, convw_ref, convb_ref, gbias_ref, hnorm_ref,
                  tail0_ref, c0_ref, m0_ref,
                  out_ref,
                  u_s, cs_s, q_s, k_s, vt_s, gr_s, br_s, cb_s, hs_s, hm_s, mg_s, c_s, m_s,
                  og_s, mz_s, gm_s):
    j = pl.program_id(1)
    tile = x_ref.shape[1]

    u_s[...] = _rmsnorm(x_ref[0], gpre_ref[...]).astype(BF16)

    @pl.when(j == 0)
    def _():
        cs_s[0:CONV_HDR, :] = tail0_ref[...]
        c_s[...] = c0_ref[...]
        m_s[...] = m0_ref[...]

    _store_gate_forms(_gates(_dot(u_s[...], wif_ref[...]) + gbias_ref[...]), gr_s, br_s, cb_s)
    for half in range(2):
        cols = slice(512 * half, 512 * (half + 1))
        cs_s[CONV_HDR:, cols] = _dot(u_s[...], wmqk_ref[:, cols])
    _store_values_ext(_dot_nt(wmvt_ref[...], u_s[...]), vt_s)

    for c in range(tile // CHUNK):
        def gate_slice():
            wcat = jnp.concatenate([wmo_ref[c], wmz_ref[c], wgm_ref[c]], axis=1)
            gates = _dot(u_s[...], wcat)
            og_s[c] = _sigmoid(gates[:, 0:ML_DV])
            mz_s[c] = _silu(gates[:, ML_DV:2 * ML_DV])
            gm_s[c] = _sigmoid(gates[:, 2 * ML_DV:])

        _conv_qk(cs_s, convw_ref, convb_ref, q_s, k_s, CHUNK * c, CHUNK)
        _mlstm_chunk_all_heads(c, gr_s, br_s, cb_s, q_s, k_s, vt_s, c_s, m_s, hs_s, gate_slice, None)
    cs_s[0:CONV_HDR, :] = cs_s[tile:tile + CONV_HDR, :]

    for r in range(tile // ROW_GROUP):
        rows = slice(ROW_GROUP * r, ROW_GROUP * (r + 1))
        for h in range(ML_HEADS):
            cols = slice(ML_DV * h, ML_DV * (h + 1))
            hg = og_s[h, rows, :] * hs_s[rows, cols]
            hc = hg - jnp.mean(hg, axis=-1, keepdims=True)
            y = hc * lax.rsqrt(jnp.mean(hc * hc, axis=-1, keepdims=True) + LN_EPS) * hnorm_ref[:, cols]
            hm_s[rows, cols] = (y * mz_s[h, rows, :]).astype(BF16)
        for h in range(ML_HEADS):
            cols = slice(ML_DV * h, ML_DV * (h + 1))
            ym = _dot(hm_s[rows, :], wmlo_ref[:, cols])
            mg_s[rows, cols] = (ya_ref[0, rows, cols].astype(F32) + gm_s[h, rows, :] * ym).astype(BF16)
        o = _dot(mg_s[rows, :], wout_ref[...])
        out_ref[0, rows, :] = x_ref[0, rows, :] + _rmsnorm(o, gpost_ref[...])


def _rope_tables(pos):
    half = ROPE_DIM // 2
    inv_freq = ROPE_THETA ** (-jnp.arange(0, ROPE_DIM, 2, dtype=F32) / ROPE_DIM)
    ang = pos[:, None] * inv_freq[None, :]
    cos, sin = jnp.cos(ang), jnp.sin(ang)
    ones = jnp.ones((pos.shape[0], HEAD_DIM - ROPE_DIM), F32)
    zeros_h = jnp.zeros((pos.shape[0], half), F32)
    zeros_r = jnp.zeros_like(ones)
    c_head = jnp.concatenate([cos, cos, ones], axis=1)
    s1_head = jnp.concatenate([-sin, zeros_h, zeros_r], axis=1)
    s2_head = jnp.concatenate([zeros_h, sin, zeros_r], axis=1)
    rep = LANES // HEAD_DIM
    return (jnp.tile(c_head, (1, rep)), jnp.tile(s1_head, (1, rep)), jnp.tile(s2_head, (1, rep)))


def _band_bias():
    kj = np.arange(2 * ATTN_BLOCK)[:, None]
    qi = np.arange(ATTN_BLOCK)[None, :]
    vis = (kj > qi) & (kj <= qi + ATTN_BLOCK)
    meta = np.zeros((N_META, ATTN_BLOCK), np.float32)
    bt = np.concatenate([meta, np.where(vis, 0.0, NEG_BIG)], axis=0).astype(np.float32)
    bt0 = np.concatenate([meta, np.where(vis & (kj >= ATTN_BLOCK), 0.0, NEG_BIG)], axis=0).astype(np.float32)
    return jnp.asarray(np.tile(bt, (1, 2))), jnp.asarray(np.tile(bt0, (1, 2)))


def _full(shape, single=True):
    nd = len(shape)
    kw = {"pipeline_mode": pl.Buffered(1)} if single else {}
    return pl.BlockSpec(shape, lambda *_: (0,) * nd, **kw)


def kernel(x, meta_tokens, norm_pre, w_in, attn_sinks, conv_w, conv_b, mlstm_gate_bias, mlstm_head_norm,
           w_attn_out, w_mlstm_out, w_out, norm_post):
    B, S, D = x.shape
    assert D == D_MODEL and w_in.shape[0] == 1 and meta_tokens.shape == (N_META, D_MODEL)
    tile = min(SEQ_TILE, S)
    assert S % tile == 0 and tile % CHUNK == 0 and tile % ATTN_BLOCK == 0
    nt = S // tile
    nblk = tile // ATTN_BLOCK
    nchunk = tile // CHUNK

    w = w_in[0].astype(BF16)
    o = 0
    parts = {}
    for name, width in (("q", 1024), ("k", 256), ("v", 256), ("z", 1024), ("mqk", 1024), ("mv", 1024),
                        ("mi", 4), ("mf", 4), ("mo", 1024), ("mz", 1024), ("ga", 1024), ("gm", 1024)):
        parts[name] = w[:, o:o + width]
        o += width
    nslice = tile // ATTN_BLOCK
    assert nslice == nchunk == ML_HEADS and D // nslice == ML_DV

    def sliced(wm):
        return wm.reshape(D, nslice, wm.shape[1] // nslice).transpose(1, 0, 2)

    wvt = parts["v"].T
    wmvt = parts["mv"].T
    wif = jnp.concatenate([parts["mi"], parts["mf"],
                           jnp.zeros((D, LANES - 2 * ML_HEADS), BF16)], axis=1)
    gbias = jnp.concatenate([mlstm_gate_bias[0].astype(F32),
                             jnp.zeros((LANES - 2 * ML_HEADS,), F32)])[None, :]
    gpre = norm_pre[0].astype(F32)[None, :]
    gpost = norm_post[0].astype(F32)[None, :]
    convw = conv_w[0].astype(F32)
    convb = conv_b[0].astype(F32)[None, :]
    hnorm = mlstm_head_norm[0].astype(F32)[None, :]
    wao = w_attn_out[0].astype(BF16)
    wmlo = w_mlstm_out[0].astype(BF16)
    wout = w_out[0].astype(BF16)
    sinks = attn_sinks[0].astype(F32)
    sink_rows = jnp.broadcast_to(jnp.repeat(sinks.reshape(N_HEADS // 2, 2), ATTN_BLOCK, axis=1)[:, None, :],
                                 (N_HEADS // 2, 8, 2 * ATTN_BLOCK))

    xm = jnp.concatenate([jnp.zeros((CHUNK - N_META, D), F32), meta_tokens.astype(F32)], axis=0)
    mpos = jnp.maximum(jnp.arange(CHUNK, dtype=F32) - (CHUNK - N_META), 0.0)
    mrc, mrs1, mrs2 = _rope_tables(mpos)
    kv_w = N_KV * HEAD_DIM
    meta_out = pl.pallas_call(
        _meta_kernel,
        out_shape=(jax.ShapeDtypeStruct((CHUNK, kv_w), F32), jax.ShapeDtypeStruct((CHUNK, kv_w), F32),
                   jax.ShapeDtypeStruct((CONV_HDR, D), F32),
                   jax.ShapeDtypeStruct((ML_HEADS, ML_EXT, ML_DK), F32),
                   jax.ShapeDtypeStruct((8, LANES), F32)),
        scratch_shapes=[pltpu.VMEM((CONV_HDR + CHUNK, D), F32), pltpu.VMEM((CHUNK, D // 2), BF16),
                        pltpu.VMEM((CHUNK, D // 2), BF16), pltpu.VMEM((ML_HEADS, 1, ML_EXT, CHUNK), F32),
                        pltpu.VMEM((1, 8, CHUNK), F32), pltpu.VMEM((1, 8, CHUNK), F32),
                        pltpu.VMEM((1, CHUNK, LANES), F32)],
        compiler_params=pltpu.CompilerParams(vmem_limit_bytes=VMEM_LIMIT),
        name="meta_tokens",
    )(xm, gpre, parts["k"], parts["v"], parts["mqk"], wmvt, wif, convw, convb, gbias,
      mrc, mrs1, mrs2)
    kmeta, vmeta, tail0, c0, m0 = meta_out
    km = kmeta[CHUNK - N_META:].reshape(N_META, N_KV, HEAD_DIM).transpose(1, 0, 2)
    km2 = jnp.concatenate([km, km], axis=2).astype(BF16)
    vm = vmeta[CHUNK - N_META:].reshape(N_META, N_KV, HEAD_DIM).transpose(1, 2, 0)
    vmt = jnp.concatenate([vm, jnp.zeros((N_KV, HEAD_DIM, LANES - N_META), F32)], axis=2).astype(BF16)

    rc, rs1, rs2 = _rope_tables(jnp.arange(S, dtype=F32) + N_META)
    bt, bt0 = _band_bias()
    x_spec = pl.BlockSpec((1, tile, D), lambda b, j: (b, j, 0))
    rope_spec = pl.BlockSpec((tile, LANES), lambda b, j: (j, 0))
    seq_params = pltpu.CompilerParams(dimension_semantics=("arbitrary", "arbitrary"),
                                      vmem_limit_bytes=VMEM_LIMIT)
    ya = pl.pallas_call(
        _attn_kernel,
        grid=(B, nt),
        in_specs=[x_spec, _full((1, D)),
                  _full((D, 1024)), _full((D, kv_w)), _full((kv_w, D)), _full((nslice, D, ML_DV)),
                  _full((nslice, D, ML_DV)), _full((D, D)),
                  rope_spec, rope_spec, rope_spec,
                  _full((N_KV, N_META, LANES)), _full((N_KV, HEAD_DIM, LANES)),
                  _full((BAND, 2 * ATTN_BLOCK)), _full((BAND, 2 * ATTN_BLOCK)),
                  _full((N_HEADS // 2, 8, 2 * ATTN_BLOCK))],
        out_specs=pl.BlockSpec((1, tile, D), lambda b, j: (b, j, 0)),
        out_shape=jax.ShapeDtypeStruct((B, S, D), BF16),
        scratch_shapes=[pltpu.VMEM((tile, D), BF16),
                        pltpu.VMEM((N_HEADS // 2, 2, tile, LANES), BF16),
                        pltpu.VMEM((N_KV, ATTN_BLOCK + tile, LANES), BF16),
                        pltpu.VMEM((N_KV, nblk + 1, HEAD_DIM, ATTN_BLOCK), BF16),
                        pltpu.VMEM((N_HEADS // 2, tile, LANES), F32),
                        pltpu.VMEM((tile, D), BF16),
                        pltpu.VMEM((nslice, tile, ML_DV), F32),
                        pltpu.VMEM((nslice, tile, ML_DV), F32)],
        compiler_params=seq_params,
        name="attn_branch",
    )(x, gpre, parts["q"], parts["k"], wvt, sliced(parts["z"]), sliced(parts["ga"]), wao,
      rc, rs1, rs2, km2, vmt, bt, bt0, sink_rows)

    out = pl.pallas_call(
        _mlstm_kernel,
        grid=(B, nt),
        in_specs=[x_spec, pl.BlockSpec((1, tile, D), lambda b, j: (b, j, 0)),
                  _full((1, D)), _full((1, D)),
                  _full((D, 1024)), _full((1024, D)), _full((D, LANES)), _full((nslice, D, ML_DV)),
                  _full((nslice, D, ML_DV)), _full((nslice, D, ML_DV)), _full((D, D)), _full((D, D)),
                  _full((CONV_W, D)), _full((1, D)), _full((1, LANES)), _full((1, D)),
                  _full((CONV_HDR, D)), _full((ML_HEADS, ML_EXT, ML_DK)), _full((8, LANES))],
        out_specs=pl.BlockSpec((1, tile, D), lambda b, j: (b, j, 0)),
        out_shape=jax.ShapeDtypeStruct((B, S, D), x.dtype),
        scratch_shapes=[pltpu.VMEM((tile, D), BF16),
                        pltpu.VMEM((CONV_HDR + tile, D), F32),
                        pltpu.VMEM((tile, D // 2), BF16),
                        pltpu.VMEM((tile, D // 2), BF16),
                        pltpu.VMEM((ML_HEADS, nchunk, ML_EXT, CHUNK), F32),
                        pltpu.VMEM((nchunk, 8, CHUNK), F32),
                        pltpu.VMEM((nchunk, 8, CHUNK), F32),
                        pltpu.VMEM((nchunk, CHUNK, LANES), F32),
                        pltpu.VMEM((tile, D), F32),
                        pltpu.VMEM((tile, D), BF16),
                        pltpu.VMEM((tile, D), BF16),
                        pltpu.VMEM((ML_HEADS, ML_EXT, ML_DK), F32),
                        pltpu.VMEM((8, LANES), F32),
                        pltpu.VMEM((nslice, tile, ML_DV), F32),
                        pltpu.VMEM((nslice, tile, ML_DV), F32),
                        pltpu.VMEM((nslice, tile, ML_DV), F32)],
        compiler_params=seq_params,
        name="mlstm_merge_out",
    )(x, ya, gpre, gpost, parts["mqk"], wmvt, wif, sliced(parts["mo"]), sliced(parts["mz"]),
      sliced(parts["gm"]),
      wmlo, wout, convw, convb, gbias, hnorm, tail0, c0, m0)
    return out
```

```python
import numpy as np
import jax
import jax.numpy as jnp
from jax import lax
from jax.experimental import pallas as pl
from jax.experimental.pallas import tpu as pltpu

F32 = jnp.float32
BF16 = jnp.bfloat16

D_MODEL = 1024
N_META = 16
HEAD_DIM = 64
N_HEADS = 16
N_KV = 4
ATTN_BLOCK = 128
BAND = N_META + 2 * ATTN_BLOCK
ROPE_DIM = 16
ROPE_THETA = 500000.0
ML_HEADS = 4
ML_DV = 256
ML_DK = 128
ML_EXT = ML_DV + 16
CONV_W = 4
CHUNK = 128
RMS_EPS = 1e-6
LN_EPS = 1e-6
NEG_BIG = -1e30
LANES = 128
CONV_HDR = 8
SEQ_TILE = 512
ROW_GROUP = 256
VMEM_LIMIT = 56 * 1024 * 1024


def _dot(a, b):
    return jnp.dot(a, b, preferred_element_type=F32)


def _dot_nt(a, b):
    return lax.dot_general(a, b, (((1,), (1,)), ((), ())), preferred_element_type=F32)


def _sigmoid(x):
    return 0.5 * jnp.tanh(0.5 * x) + 0.5


def _silu(x):
    return x * _sigmoid(x)


def _log_sigmoid(x):
    return jnp.minimum(x, 0.0) - jnp.log(1.0 + jnp.exp(-jnp.abs(x)))


def _rmsnorm(xf, g):
    ms = jnp.mean(xf * xf, axis=-1, keepdims=True)
    return xf * lax.rsqrt(ms + RMS_EPS) * g


def _rope(blk, c_t, s1, s2):
    return blk * c_t + pltpu.roll(blk, LANES - 8, 1) * s1 + pltpu.roll(blk, 8, 1) * s2


def _causal_conv_silu(cs_ref, w_ref, b_ref, row0, rows, cols):
    acc = b_ref[:, cols]
    for tap in range(CONV_W):
        off = CONV_HDR - (CONV_W - 1) + tap + row0
        acc = acc + w_ref[tap:tap + 1, cols] * cs_ref[off:off + rows, cols]
    return _silu(acc)


def _conv_qk(cs_ref, w_ref, b_ref, q_ref, k_ref, row0, rows, valid=None):
    half = D_MODEL // 2
    for cc in range(4):
        cols = slice(256 * cc, 256 * (cc + 1))
        act = _causal_conv_silu(cs_ref, w_ref, b_ref, row0, rows, cols)
        if valid is not None:
            act = jnp.where(valid, act, 0.0)
        if cc < 2:
            q_ref[row0:row0 + rows, cols] = act.astype(BF16)
        else:
            k_ref[row0:row0 + rows, 256 * cc - half:256 * (cc + 1) - half] = (
                act * (ML_DK ** -0.5)).astype(BF16)


def _gate_rows(graw, gbias_ref, valid_t=None):
    g = (graw + gbias_ref[...])[0:8, :]
    sub = lax.broadcasted_iota(jnp.int32, g.shape, 0)
    grow = jnp.where(sub < ML_HEADS, g, _log_sigmoid(g))
    if valid_t is not None:
        grow = jnp.where(valid_t, grow, jnp.where(sub < ML_HEADS, NEG_BIG, 0.0))
    return grow


def _store_gate_forms(c, grow, gr_ref, br_ref, cb_ref):
    lane = lax.broadcasted_iota(jnp.int32, grow.shape, 1)
    brow = grow
    sh = 1
    while sh < CHUNK:
        brow = brow + jnp.where(lane >= sh, pltpu.roll(brow, sh, 1), 0.0)
        sh *= 2
    diff = brow[ML_HEADS:2 * ML_HEADS, :] - grow[0:ML_HEADS, :]
    pad = jnp.zeros((LANES - ML_HEADS, CHUNK), F32)
    gr_ref[c], br_ref[c], cb_ref[c] = grow, brow, jnp.concatenate([diff, pad], axis=0).T


def _mlstm_chunk_all_heads(c, gr_ref, br_ref, cb_ref, q_ref, k_ref, vt_ref, c_ref, m_ref, hs_ref,
                           between=None):
    r0 = c * CHUNK if isinstance(c, int) else pl.multiple_of(c * CHUNK, CHUNK)
    n = CHUNK
    grow, brow, cbm = gr_ref[c], br_ref[c], cb_ref[c]
    row = lax.broadcasted_iota(jnp.int32, (n, 2 * n), 0)
    col = lax.broadcasted_iota(jnp.int32, (n, 2 * n), 1)
    tri2 = row <= (col & (n - 1))
    lo_row = lax.broadcasted_iota(jnp.int32, (1, 2 * n), 1) < n
    lo = lax.broadcasted_iota(jnp.int32, (n, 2 * ML_DK), 1) < ML_DK

    def block_diag(x2):
        zero = jnp.zeros_like(x2)
        return jnp.concatenate([jnp.where(lo, x2, zero), jnp.where(lo, zero, x2)], axis=0)

    staged = []
    for p in range(ML_HEADS // 2):
        a, b = 2 * p, 2 * p + 1
        cols = slice(2 * ML_DK * p, 2 * ML_DK * (p + 1))
        q2, k2, vt2, c2 = q_ref[pl.ds(r0, n), cols], k_ref[pl.ds(r0, n), cols], vt_ref[p, c], c_ref[p]
        li2 = jnp.concatenate([grow[a:a + 1, :], grow[b:b + 1, :]], axis=1)
        b2 = jnp.concatenate([brow[ML_HEADS + a:ML_HEADS + a + 1, :],
                              brow[ML_HEADS + b:ML_HEADS + b + 1, :]], axis=1)
        m_a, m_b = m_ref[a:a + 1, 0:1], m_ref[b:b + 1, 0:1]
        m_prev2 = jnp.where(lo_row, m_a, m_b)
        bl_a, bl_b = b2[:, n - 1:n], b2[:, 2 * n - 1:2 * n]
        b_last2 = jnp.where(lo_row, bl_a, bl_b)
        sc = _dot_nt(jnp.concatenate([k2, c2.astype(BF16)], axis=0), block_diag(q2))
        a_row2 = b_last2 - b2 + li2
        mn_a = jnp.maximum(bl_a + m_a, jnp.max(a_row2[:, :n], axis=1, keepdims=True))
        mn_b = jnp.maximum(bl_b + m_b, jnp.max(a_row2[:, n:], axis=1, keepdims=True))
        m_new2 = jnp.where(lo_row, mn_a, mn_b)
        w_c2 = jnp.exp(b_last2 + m_prev2 - m_new2)
        w_a2 = jnp.exp(a_row2 - m_new2)
        c_ref[p] = w_c2 * c2 + _dot((vt2 * w_a2).astype(BF16), block_diag(k2))
        m_ref[a:a + 1, :] = jnp.broadcast_to(mn_a, (1, LANES))
        m_ref[b:b + 1, :] = jnp.broadcast_to(mn_b, (1, LANES))
        staged.append((sc, vt2, b2, m_prev2))
    if between is not None:
        between()
    if hs_ref is None:
        return
    for p, (sc, vt2, b2, m_prev2) in enumerate(staged):
        a, b = 2 * p, 2 * p + 1
        cb2 = jnp.concatenate([jnp.broadcast_to(cbm[:, a:a + 1], (n, n)),
                               jnp.broadcast_to(cbm[:, b:b + 1], (n, n))], axis=1)
        dt = jnp.where(tri2, b2 - cb2, NEG_BIG)
        inter2 = b2 + m_prev2
        m_row2 = jnp.maximum(inter2, jnp.max(dt, axis=0, keepdims=True))
        sct2 = (sc[0:n] * jnp.exp(dt - m_row2)).astype(BF16)
        het = jnp.exp(inter2 - m_row2) * sc[n:] + _dot(vt2.astype(BF16), block_diag(sct2))
        den = het[ML_DV:ML_DV + 1, :]
        ht2 = het[0:ML_DV, :] / jnp.maximum(jnp.abs(den), jnp.exp(-m_row2))
        hs_ref[pl.ds(r0, n), ML_DV * a:ML_DV * (a + 1)] = ht2[:, :n].T
        hs_ref[pl.ds(r0, n), ML_DV * b:ML_DV * (b + 1)] = ht2[:, n:].T


def _store_values_ext(vt_all, vt_ref):
    nchunk = vt_all.shape[1] // CHUNK
    sub = lax.broadcasted_iota(jnp.int32, (ML_EXT - ML_DV, 2 * CHUNK), 0)
    tail = jnp.where(sub == 0, 1.0, 0.0)
    for p in range(ML_HEADS // 2):
        for c in range(nchunk):
            tok = slice(CHUNK * c, CHUNK * (c + 1))
            for hh in range(2):
                h = 2 * p + hh
                vt_ref[p, c, 0:ML_DV, CHUNK * hh:CHUNK * (hh + 1)] = vt_all[ML_DV * h:ML_DV * (h + 1), tok]
            vt_ref[p, c, ML_DV:ML_EXT, :] = tail


def _meta_kernel(xm_ref, gpre_ref, wk_ref, wv_ref, wmqk_ref, wmvt_ref,
                 convw_ref, convb_ref, gbias_ref, rc_ref, rs1_ref, rs2_ref,
                 kmeta_ref, vmeta_ref, tail_ref, c0_ref, m0_ref,
                 cs_s, q_s, k_s, vt_s, gr_s, br_s, cb_s):
    rows = xm_ref.shape[0]
    u = _rmsnorm(xm_ref[...], gpre_ref[...]).astype(BF16)
    valid = lax.broadcasted_iota(jnp.int32, (rows, 1), 0) >= rows - N_META
    valid_t = lax.broadcasted_iota(jnp.int32, (1, rows), 1) >= rows - N_META

    kc = _dot(u, wk_ref[...])
    for cb in range(kc.shape[1] // LANES):
        sl = slice(LANES * cb, LANES * (cb + 1))
        kmeta_ref[:, sl] = _rope(kc[:, sl], rc_ref[...], rs1_ref[...], rs2_ref[...])
    vmeta_ref[...] = _dot(u, wv_ref[...])

    cs_s[0:CONV_HDR, :] = jnp.zeros((CONV_HDR, D_MODEL), F32)
    cs_s[CONV_HDR:, :] = _dot(u, wmqk_ref[...])
    tail_ref[...] = cs_s[rows:rows + CONV_HDR, :]
    _conv_qk(cs_s, convw_ref, convb_ref, q_s, k_s, 0, rows, valid)

    vt_all = _dot_nt(wmvt_ref[...], u)
    _store_values_ext(jnp.where(valid_t, vt_all, 0.0), vt_s)
    _store_gate_forms(0, _gate_rows(vt_all[ML_HEADS * ML_DV:, :], gbias_ref, valid_t), gr_s, br_s, cb_s)

    c0_ref[...] = jnp.zeros(c0_ref.shape, F32)
    m0_ref[...] = jnp.zeros(m0_ref.shape, F32)
    _mlstm_chunk_all_heads(0, gr_s, br_s, cb_s, q_s, k_s, vt_s, c0_ref, m0_ref, None)


def _attn_kernel(x_ref, gpre_ref, wq_ref, wk_ref, wvt_ref, wz_ref, wga_ref, wao_ref,
                 rc_ref, rs1_ref, rs2_ref, km2_ref, vmt_ref, bt_ref, bt0_ref, sink_ref,
                 out_ref,
                 u_s, q_s, kk_s, vt_s, a_s, gated_s, zg_s, ga_s):
    j = pl.program_id(1)
    tile = x_ref.shape[1]
    nblk = tile // ATTN_BLOCK
    pairs = N_HEADS // 2

    @pl.when(j == 0)
    def _():
        kk_s[:, 0:ATTN_BLOCK, :] = jnp.zeros((N_KV, ATTN_BLOCK, LANES), BF16)
        vt_s[:, 0] = jnp.zeros((N_KV, HEAD_DIM, ATTN_BLOCK), BF16)

    u_s[...] = _rmsnorm(x_ref[0], gpre_ref[...]).astype(BF16)

    c_t, s1, s2 = rc_ref[...], rs1_ref[...], rs2_ref[...]
    low = lax.broadcasted_iota(jnp.int32, (tile, LANES), 1) < HEAD_DIM
    for half in range(2):
        qc = _dot(u_s[...], wq_ref[:, 512 * half:512 * (half + 1)])
        for cc in range(4):
            blk = _rope(qc[:, LANES * cc:LANES * (cc + 1)], c_t, s1, s2) * (HEAD_DIM ** -0.5)
            q_s[4 * half + cc, 0] = jnp.where(low, blk, 0.0).astype(BF16)
            q_s[4 * half + cc, 1] = jnp.where(low, 0.0, blk).astype(BF16)

    kc = _dot(u_s[...], wk_ref[...])
    for cb in range(N_KV // 2):
        blk = _rope(kc[:, LANES * cb:LANES * (cb + 1)], c_t, s1, s2)
        swapped = pltpu.roll(blk, HEAD_DIM, 1)
        kk_s[2 * cb, ATTN_BLOCK:, :] = jnp.where(low, blk, swapped).astype(BF16)
        kk_s[2 * cb + 1, ATTN_BLOCK:, :] = jnp.where(low, swapped, blk).astype(BF16)
    vt_all = _dot_nt(wvt_ref[...], u_s[...])
    for g in range(N_KV):
        for bi in range(nblk):
            vt_s[g, 1 + bi] = vt_all[HEAD_DIM * g:HEAD_DIM * (g + 1),
                                     ATTN_BLOCK * bi:ATTN_BLOCK * (bi + 1)].astype(BF16)

    def block_scores(i):
        r0 = i * ATTN_BLOCK
        out = []
        for c in range(pairs):
            g = c // 2
            qst = jnp.concatenate([q_s[c, 0, r0:r0 + ATTN_BLOCK, :],
                                   q_s[c, 1, r0:r0 + ATTN_BLOCK, :]], axis=0)
            band = jnp.concatenate([km2_ref[g], kk_s[g, r0:r0 + 2 * ATTN_BLOCK, :]], axis=0)
            out.append(_dot_nt(band, qst))
        return out

    def gate_slice(i):
        for r in range(tile // ROW_GROUP):
            rows = slice(ROW_GROUP * r, ROW_GROUP * (r + 1))
            zg_s[i, rows, :] = _silu(_dot(u_s[rows, :], wz_ref[i]))
            ga_s[i, rows, :] = _sigmoid(_dot(u_s[rows, :], wga_ref[i]))

    gate_slice(0)
    scores = block_scores(0)
    for i in range(nblk):
        r0 = i * ATTN_BLOCK
        bias = jnp.where(j == 0, bt0_ref[...], bt_ref[...]) if i == 0 else bt_ref[...]
        next_scores = block_scores(i + 1) if i + 1 < nblk else None
        if i + 1 < nblk:
            gate_slice(i + 1)
        probs = []
        for c in range(pairs):
            st = scores[c] + bias
            sink = sink_ref[c, 0:1, :]
            mx = jnp.maximum(jnp.max(st, axis=0, keepdims=True), sink)
            p = jnp.exp(st - mx)
            den = jnp.sum(p, axis=0, keepdims=True) + jnp.exp(sink - mx)
            probs.append((p.astype(BF16), 1.0 / den))
        for c in range(pairs):
            g = c // 2
            pb, rinv = probs[c]
            vt = jnp.concatenate([vt_s[g, i], vt_s[g, i + 1]], axis=1)
            ot = (_dot(vt, pb[N_META:]) + _dot(vmt_ref[g][:, 0:N_META], pb[0:N_META])) * rinv
            o = jnp.concatenate([ot[:, :ATTN_BLOCK], ot[:, ATTN_BLOCK:]], axis=0).T
            a_s[c, r0:r0 + ATTN_BLOCK, :] = o
        scores = next_scores

    kk_s[:, 0:ATTN_BLOCK, :] = kk_s[:, tile:tile + ATTN_BLOCK, :]
    vt_s[:, 0] = vt_s[:, nblk]

    width = zg_s.shape[2]
    per = width // LANES
    for g4 in range(nblk):
        att = jnp.concatenate([a_s[per * g4 + p] for p in range(per)], axis=1)
        gated_s[:, width * g4:width * (g4 + 1)] = (att * zg_s[g4]).astype(BF16)
    for g4 in range(nblk):
        cols = slice(width * g4, width * (g4 + 1))
        out_ref[0, :, cols] = (ga_s[g4] * _dot(gated_s[...], wao_ref[:, cols])).astype(out_ref.dtype)


def _mlstm_kernel(x_ref, ya_ref, gpre_ref, gpost_ref, wmqk_ref, wmvt_ref, wmo_ref, wmz_ref,
                  wgm_ref, wmlo_ref, wout_ref, convw_ref, convb_ref, gbias_ref, hnorm_ref,
                  tail0_ref, c0_ref, m0_ref,
                  out_ref,
                  u_s, cs_s, q_s, k_s, vt_s, gr_s, br_s, cb_s, hs_s, hm_s, mg_s, c_s, m_s,
                  og_s, mz_s, gm_s):
    j = pl.program_id(1)
    tile = x_ref.shape[1]

    @pl.when(j == 0)
    def _():
        cs_s[0:CONV_HDR, :] = tail0_ref[...]
        c_s[...] = c0_ref[...]
        m_s[...] = m0_ref[...]

    u_s[...] = _rmsnorm(x_ref[0], gpre_ref[...]).astype(BF16)

    vt_all = _dot_nt(wmvt_ref[...], u_s[...])
    for c in range(tile // CHUNK):
        graw = vt_all[ML_HEADS * ML_DV:, CHUNK * c:CHUNK * (c + 1)]
        _store_gate_forms(c, _gate_rows(graw, gbias_ref), gr_s, br_s, cb_s)
    _store_values_ext(vt_all, vt_s)
    for half in range(2):
        cols = slice(512 * half, 512 * (half + 1))
        cs_s[CONV_HDR:, cols] = _dot(u_s[...], wmqk_ref[:, cols])

    for c in range(tile // CHUNK):
        def gate_slice():
            wcat = jnp.concatenate([wmo_ref[c], wmz_ref[c], wgm_ref[c]], axis=1)
            gates = _dot(u_s[...], wcat)
            og_s[c] = _sigmoid(gates[:, 0:ML_DV])
            mz_s[c] = _silu(gates[:, ML_DV:2 * ML_DV])
            gm_s[c] = _sigmoid(gates[:, 2 * ML_DV:])

        _conv_qk(cs_s, convw_ref, convb_ref, q_s, k_s, CHUNK * c, CHUNK)
        _mlstm_chunk_all_heads(c, gr_s, br_s, cb_s, q_s, k_s, vt_s, c_s, m_s, hs_s, gate_slice)
    cs_s[0:CONV_HDR, :] = cs_s[tile:tile + CONV_HDR, :]

    for r in range(tile // ROW_GROUP):
        rows = slice(ROW_GROUP * r, ROW_GROUP * (r + 1))
        for h in range(ML_HEADS):
            cols = slice(ML_DV * h, ML_DV * (h + 1))
            hg = og_s[h, rows, :] * hs_s[rows, cols]
            hc = hg - jnp.mean(hg, axis=-1, keepdims=True)
            y = hc * lax.rsqrt(jnp.mean(hc * hc, axis=-1, keepdims=True) + LN_EPS) * hnorm_ref[:, cols]
            hm_s[rows, cols] = (y * mz_s[h, rows, :]).astype(BF16)
        for h in range(ML_HEADS):
            cols = slice(ML_DV * h, ML_DV * (h + 1))
            ym = _dot(hm_s[rows, :], wmlo_ref[:, cols])
            mg_s[rows, cols] = (ya_ref[0, rows, cols].astype(F32) + gm_s[h, rows, :] * ym).astype(BF16)
        o = _dot(mg_s[rows, :], wout_ref[...])
        out_ref[0, rows, :] = x_ref[0, rows, :] + _rmsnorm(o, gpost_ref[...])


def _rope_tables(pos):
    half = ROPE_DIM // 2
    inv_freq = ROPE_THETA ** (-jnp.arange(0, ROPE_DIM, 2, dtype=F32) / ROPE_DIM)
    ang = pos[:, None] * inv_freq[None, :]
    cos, sin = jnp.cos(ang), jnp.sin(ang)
    ones = jnp.ones((pos.shape[0], HEAD_DIM - ROPE_DIM), F32)
    zeros_h = jnp.zeros((pos.shape[0], half), F32)
    zeros_r = jnp.zeros_like(ones)
    c_head = jnp.concatenate([cos, cos, ones], axis=1)
    s1_head = jnp.concatenate([-sin, zeros_h, zeros_r], axis=1)
    s2_head = jnp.concatenate([zeros_h, sin, zeros_r], axis=1)
    rep = LANES // HEAD_DIM
    return (jnp.tile(c_head, (1, rep)), jnp.tile(s1_head, (1, rep)), jnp.tile(s2_head, (1, rep)))


def _band_bias():
    kj = np.arange(2 * ATTN_BLOCK)[:, None]
    qi = np.arange(ATTN_BLOCK)[None, :]
    vis = (kj > qi) & (kj <= qi + ATTN_BLOCK)
    meta = np.zeros((N_META, ATTN_BLOCK), np.float32)
    bt = np.concatenate([meta, np.where(vis, 0.0, NEG_BIG)], axis=0).astype(np.float32)
    bt0 = np.concatenate([meta, np.where(vis & (kj >= ATTN_BLOCK), 0.0, NEG_BIG)], axis=0).astype(np.float32)
    return jnp.asarray(np.tile(bt, (1, 2))), jnp.asarray(np.tile(bt0, (1, 2)))


def _full(shape, single=True):
    nd = len(shape)
    kw = {"pipeline_mode": pl.Buffered(1)} if single else {}
    return pl.BlockSpec(shape, lambda *_: (0,) * nd, **kw)


def kernel(x, meta_tokens, norm_pre, w_in, attn_sinks, conv_w, conv_b, mlstm_gate_bias, mlstm_head_norm,
           w_attn_out, w_mlstm_out, w_out, norm_post):
    B, S, D = x.shape
    assert D == D_MODEL and w_in.shape[0] == 1 and meta_tokens.shape == (N_META, D_MODEL)
    tile = min(SEQ_TILE, S)
    assert S % tile == 0 and tile % CHUNK == 0 and tile % ATTN_BLOCK == 0
    nt = S // tile
    nblk = tile // ATTN_BLOCK
    nchunk = tile // CHUNK

    w = w_in[0].astype(BF16)
    o = 0
    parts = {}
    for name, width in (("q", 1024), ("k", 256), ("v", 256), ("z", 1024), ("mqk", 1024), ("mv", 1024),
                        ("mi", 4), ("mf", 4), ("mo", 1024), ("mz", 1024), ("ga", 1024), ("gm", 1024)):
        parts[name] = w[:, o:o + width]
        o += width
    nslice = tile // ATTN_BLOCK
    assert nslice == nchunk == ML_HEADS and D // nslice == ML_DV

    def sliced(wm):
        return wm.reshape(D, nslice, wm.shape[1] // nslice).transpose(1, 0, 2)

    wvt = parts["v"].T
    gate_rows = 16
    wmvt = jnp.concatenate([parts["mv"].T, parts["mi"].T, parts["mf"].T,
                            jnp.zeros((gate_rows - 2 * ML_HEADS, D), BF16)], axis=0)
    gbias = jnp.broadcast_to(jnp.concatenate([mlstm_gate_bias[0].astype(F32),
                                              jnp.zeros((gate_rows - 2 * ML_HEADS,), F32)])[:, None],
                             (gate_rows, CHUNK))
    gpre = norm_pre[0].astype(F32)[None, :]
    gpost = norm_post[0].astype(F32)[None, :]
    convw = conv_w[0].astype(F32)
    convb = conv_b[0].astype(F32)[None, :]
    hnorm = mlstm_head_norm[0].astype(F32)[None, :]
    wao = w_attn_out[0].astype(BF16)
    wmlo = w_mlstm_out[0].astype(BF16)
    wout = w_out[0].astype(BF16)
    sinks = attn_sinks[0].astype(F32)
    sink_rows = jnp.broadcast_to(jnp.repeat(sinks.reshape(N_HEADS // 2, 2), ATTN_BLOCK, axis=1)[:, None, :],
                                 (N_HEADS // 2, 8, 2 * ATTN_BLOCK))

    xm = jnp.concatenate([jnp.zeros((CHUNK - N_META, D), F32), meta_tokens.astype(F32)], axis=0)
    mpos = jnp.maximum(jnp.arange(CHUNK, dtype=F32) - (CHUNK - N_META), 0.0)
    mrc, mrs1, mrs2 = _rope_tables(mpos)
    kv_w = N_KV * HEAD_DIM
    meta_out = pl.pallas_call(
        _meta_kernel,
        out_shape=(jax.ShapeDtypeStruct((CHUNK, kv_w), F32), jax.ShapeDtypeStruct((CHUNK, kv_w), F32),
                   jax.ShapeDtypeStruct((CONV_HDR, D), F32),
                   jax.ShapeDtypeStruct((ML_HEADS // 2, ML_EXT, 2 * ML_DK), F32),
                   jax.ShapeDtypeStruct((8, LANES), F32)),
        scratch_shapes=[pltpu.VMEM((CONV_HDR + CHUNK, D), F32), pltpu.VMEM((CHUNK, D // 2), BF16),
                        pltpu.VMEM((CHUNK, D // 2), BF16), pltpu.VMEM((ML_HEADS // 2, 1, ML_EXT, 2 * CHUNK), F32),
                        pltpu.VMEM((1, 8, CHUNK), F32), pltpu.VMEM((1, 8, CHUNK), F32),
                        pltpu.VMEM((1, CHUNK, LANES), F32)],
        compiler_params=pltpu.CompilerParams(vmem_limit_bytes=VMEM_LIMIT),
        name="meta_tokens",
    )(xm, gpre, parts["k"], parts["v"], parts["mqk"], wmvt, convw, convb, gbias,
      mrc, mrs1, mrs2)
    kmeta, vmeta, tail0, c0, m0 = meta_out
    km = kmeta[CHUNK - N_META:].reshape(N_META, N_KV, HEAD_DIM).transpose(1, 0, 2)
    km2 = jnp.concatenate([km, km], axis=2).astype(BF16)
    vm = vmeta[CHUNK - N_META:].reshape(N_META, N_KV, HEAD_DIM).transpose(1, 2, 0)
    vmt = jnp.concatenate([vm, jnp.zeros((N_KV, HEAD_DIM, LANES - N_META), F32)], axis=2).astype(BF16)

    rc, rs1, rs2 = _rope_tables(jnp.arange(S, dtype=F32) + N_META)
    bt, bt0 = _band_bias()
    x_spec = pl.BlockSpec((1, tile, D), lambda b, j: (b, j, 0))
    rope_spec = pl.BlockSpec((tile, LANES), lambda b, j: (j, 0))
    seq_params = pltpu.CompilerParams(dimension_semantics=("arbitrary", "arbitrary"),
                                      vmem_limit_bytes=VMEM_LIMIT)
    ya = pl.pallas_call(
        _attn_kernel,
        grid=(B, nt),
        in_specs=[x_spec, _full((1, D)),
                  _full((D, 1024)), _full((D, kv_w)), _full((kv_w, D)), _full((nslice, D, ML_DV)),
                  _full((nslice, D, ML_DV)), _full((D, D)),
                  rope_spec, rope_spec, rope_spec,
                  _full((N_KV, N_META, LANES)), _full((N_KV, HEAD_DIM, LANES)),
                  _full((BAND, 2 * ATTN_BLOCK)), _full((BAND, 2 * ATTN_BLOCK)),
                  _full((N_HEADS // 2, 8, 2 * ATTN_BLOCK))],
        out_specs=pl.BlockSpec((1, tile, D), lambda b, j: (b, j, 0)),
        out_shape=jax.ShapeDtypeStruct((B, S, D), BF16),
        scratch_shapes=[pltpu.VMEM((tile, D), BF16),
                        pltpu.VMEM((N_HEADS // 2, 2, tile, LANES), BF16),
                        pltpu.VMEM((N_KV, ATTN_BLOCK + tile, LANES), BF16),
                        pltpu.VMEM((N_KV, nblk + 1, HEAD_DIM, ATTN_BLOCK), BF16),
                        pltpu.VMEM((N_HEADS // 2, tile, LANES), F32),
                        pltpu.VMEM((tile, D), BF16),
                        pltpu.VMEM((nslice, tile, ML_DV), F32),
                        pltpu.VMEM((nslice, tile, ML_DV), F32)],
        compiler_params=seq_params,
        name="attn_branch",
    )(x, gpre, parts["q"], parts["k"], wvt, sliced(parts["z"]), sliced(parts["ga"]), wao,
      rc, rs1, rs2, km2, vmt, bt, bt0, sink_rows)

    out = pl.pallas_call(
        _mlstm_kernel,
        grid=(B, nt),
        in_specs=[x_spec, pl.BlockSpec((1, tile, D), lambda b, j: (b, j, 0)),
                  _full((1, D)), _full((1, D)),
                  _full((D, 1024)), _full((1024 + gate_rows, D)), _full((nslice, D, ML_DV)),
                  _full((nslice, D, ML_DV)), _full((nslice, D, ML_DV)), _full((D, D)), _full((D, D)),
                  _full((CONV_W, D)), _full((1, D)), _full((gate_rows, CHUNK)), _full((1, D)),
                  _full((CONV_HDR, D)), _full((ML_HEADS // 2, ML_EXT, 2 * ML_DK)), _full((8, LANES))],
        out_specs=pl.BlockSpec((1, tile, D), lambda b, j: (b, j, 0)),
        out_shape=jax.ShapeDtypeStruct((B, S, D), x.dtype),
        scratch_shapes=[pltpu.VMEM((tile, D), BF16),
                        pltpu.VMEM((CONV_HDR + tile, D), F32),
                        pltpu.VMEM((tile, D // 2), BF16),
                        pltpu.VMEM((tile, D // 2), BF16),
                        pltpu.VMEM((ML_HEADS // 2, nchunk, ML_EXT, 2 * CHUNK), F32),
                        pltpu.VMEM((nchunk, 8, CHUNK), F32),
                        pltpu.VMEM((nchunk, 8, CHUNK), F32),
                        pltpu.VMEM((nchunk, CHUNK, LANES), F32),
                        pltpu.VMEM((tile, D), F32),
                        pltpu.VMEM((tile, D), BF16),
                        pltpu.VMEM((tile, D), BF16),
                        pltpu.VMEM((ML_HEADS // 2, ML_EXT, 2 * ML_DK), F32),
                        pltpu.VMEM((8, LANES), F32),
                        pltpu.VMEM((nslice, tile, ML_DV), F32),
                        pltpu.VMEM((nslice, tile, ML_DV), F32),
                        pltpu.VMEM((nslice, tile, ML_DV), F32)],
        compiler_params=seq_params,
        name="mlstm_merge_out",
    )(x, ya, gpre, gpost, parts["mqk"], wmvt, sliced(parts["mo"]), sliced(parts["mz"]),
      sliced(parts["gm"]),
      wmlo, wout, convw, convb, gbias, hnorm, tail0, c0, m0)
    return out
```

```python
import numpy as np
import jax
import jax.numpy as jnp
from jax import lax
from jax.experimental import pallas as pl
from jax.experimental.pallas import tpu as pltpu

F32 = jnp.float32
BF16 = jnp.bfloat16

D_MODEL = 1024
N_META = 16
HEAD_DIM = 64
N_HEADS = 16
N_KV = 4
ATTN_BLOCK = 128
BAND = N_META + 2 * ATTN_BLOCK
ROPE_DIM = 16
ROPE_THETA = 500000.0
ML_HEADS = 4
ML_DV = 256
ML_DK = 128
ML_EXT = ML_DV + 16
CONV_W = 4
CHUNK = 128
RMS_EPS = 1e-6
LN_EPS = 1e-6
NEG_BIG = -1e30
LANES = 128
CONV_HDR = 8
SEQ_TILE = 512
ROW_GROUP = 256
VMEM_LIMIT = 56 * 1024 * 1024


def _dot(a, b):
    return jnp.dot(a, b, preferred_element_type=F32)


def _dot_nt(a, b):
    return lax.dot_general(a, b, (((1,), (1,)), ((), ())), preferred_element_type=F32)


def _sigmoid(x):
    return 0.5 * jnp.tanh(0.5 * x) + 0.5


def _silu(x):
    return x * _sigmoid(x)


def _log_sigmoid(x):
    return jnp.minimum(x, 0.0) - jnp.log(1.0 + jnp.exp(-jnp.abs(x)))


def _rmsnorm(xf, g):
    ms = jnp.mean(xf * xf, axis=-1, keepdims=True)
    return xf * lax.rsqrt(ms + RMS_EPS) * g


def _rope(blk, c_t, s1, s2):
    return blk * c_t + pltpu.roll(blk, LANES - 8, 1) * s1 + pltpu.roll(blk, 8, 1) * s2


def _causal_conv_silu(cs_ref, w_ref, b_ref, row0, rows, cols):
    acc = b_ref[:, cols]
    for tap in range(CONV_W):
        off = CONV_HDR - (CONV_W - 1) + tap + row0
        acc = acc + w_ref[tap:tap + 1, cols] * cs_ref[off:off + rows, cols]
    return _silu(acc)


def _conv_qk(cs_ref, w_ref, b_ref, q_ref, k_ref, row0, rows, valid=None):
    half = D_MODEL // 2
    for cc in range(4):
        cols = slice(256 * cc, 256 * (cc + 1))
        act = _causal_conv_silu(cs_ref, w_ref, b_ref, row0, rows, cols)
        if valid is not None:
            act = jnp.where(valid, act, 0.0)
        if cc < 2:
            q_ref[row0:row0 + rows, cols] = act.astype(BF16)
        else:
            k_ref[row0:row0 + rows, 256 * cc - half:256 * (cc + 1) - half] = (
                act * (ML_DK ** -0.5)).astype(BF16)


def _gate_rows(graw, gbias_ref, valid_t=None):
    g = (graw + gbias_ref[...])[0:8, :]
    sub = lax.broadcasted_iota(jnp.int32, g.shape, 0)
    grow = jnp.where(sub < ML_HEADS, g, _log_sigmoid(g))
    if valid_t is not None:
        grow = jnp.where(valid_t, grow, jnp.where(sub < ML_HEADS, NEG_BIG, 0.0))
    return grow


def _store_gate_forms(c, grow, gr_ref, br_ref, cb_ref):
    lane = lax.broadcasted_iota(jnp.int32, grow.shape, 1)
    brow = grow
    sh = 1
    while sh < CHUNK:
        brow = brow + jnp.where(lane >= sh, pltpu.roll(brow, sh, 1), 0.0)
        sh *= 2
    diff = brow[ML_HEADS:2 * ML_HEADS, :] - grow[0:ML_HEADS, :]
    pad = jnp.zeros((LANES - ML_HEADS, CHUNK), F32)
    gr_ref[c], br_ref[c], cb_ref[c] = grow, brow, jnp.concatenate([diff, pad], axis=0).T


def _mlstm_chunk_all_heads(c, gr_ref, br_ref, cb_ref, q_ref, k_ref, vt_ref, c_ref, m_ref, hs_ref,
                           between=None):
    r0 = c * CHUNK if isinstance(c, int) else pl.multiple_of(c * CHUNK, CHUNK)
    n = CHUNK
    grow, brow, cbm = gr_ref[c], br_ref[c], cb_ref[c]
    row = lax.broadcasted_iota(jnp.int32, (n, 2 * n), 0)
    col = lax.broadcasted_iota(jnp.int32, (n, 2 * n), 1)
    tri2 = row <= (col & (n - 1))
    lo_row = lax.broadcasted_iota(jnp.int32, (1, 2 * n), 1) < n
    lo = lax.broadcasted_iota(jnp.int32, (n, 2 * ML_DK), 1) < ML_DK

    def block_diag(x2):
        zero = jnp.zeros_like(x2)
        return jnp.concatenate([jnp.where(lo, x2, zero), jnp.where(lo, zero, x2)], axis=0)

    staged = []
    for p in range(ML_HEADS // 2):
        a, b = 2 * p, 2 * p + 1
        cols = slice(2 * ML_DK * p, 2 * ML_DK * (p + 1))
        q2, k2, vt2, c2 = q_ref[pl.ds(r0, n), cols], k_ref[pl.ds(r0, n), cols], vt_ref[p, c], c_ref[p]
        li2 = jnp.concatenate([grow[a:a + 1, :], grow[b:b + 1, :]], axis=1)
        b2 = jnp.concatenate([brow[ML_HEADS + a:ML_HEADS + a + 1, :],
                              brow[ML_HEADS + b:ML_HEADS + b + 1, :]], axis=1)
        m_a, m_b = m_ref[a:a + 1, 0:1], m_ref[b:b + 1, 0:1]
        m_prev2 = jnp.where(lo_row, m_a, m_b)
        bl_a, bl_b = b2[:, n - 1:n], b2[:, 2 * n - 1:2 * n]
        b_last2 = jnp.where(lo_row, bl_a, bl_b)
        sc = _dot_nt(jnp.concatenate([k2, c2.astype(BF16)], axis=0), block_diag(q2))
        a_row2 = b_last2 - b2 + li2
        mn_a = jnp.maximum(bl_a + m_a, jnp.max(a_row2[:, :n], axis=1, keepdims=True))
        mn_b = jnp.maximum(bl_b + m_b, jnp.max(a_row2[:, n:], axis=1, keepdims=True))
        m_new2 = jnp.where(lo_row, mn_a, mn_b)
        w_c2 = jnp.exp(b_last2 + m_prev2 - m_new2)
        w_a2 = jnp.exp(a_row2 - m_new2)
        c_ref[p] = w_c2 * c2 + _dot((vt2 * w_a2).astype(BF16), block_diag(k2))
        m_ref[a:a + 1, :] = jnp.broadcast_to(mn_a, (1, LANES))
        m_ref[b:b + 1, :] = jnp.broadcast_to(mn_b, (1, LANES))
        staged.append((sc, vt2, b2, m_prev2))
    if between is not None:
        between()
    if hs_ref is None:
        return
    for p, (sc, vt2, b2, m_prev2) in enumerate(staged):
        a, b = 2 * p, 2 * p + 1
        cb2 = jnp.concatenate([jnp.broadcast_to(cbm[:, a:a + 1], (n, n)),
                               jnp.broadcast_to(cbm[:, b:b + 1], (n, n))], axis=1)
        dt = jnp.where(tri2, b2 - cb2, NEG_BIG)
        inter2 = b2 + m_prev2
        m_row2 = jnp.maximum(inter2, jnp.max(dt, axis=0, keepdims=True))
        sct2 = (sc[0:n] * jnp.exp(dt - m_row2)).astype(BF16)
        het = jnp.exp(inter2 - m_row2) * sc[n:] + _dot(vt2.astype(BF16), block_diag(sct2))
        den = het[ML_DV:ML_DV + 1, :]
        ht2 = het[0:ML_DV, :] / jnp.maximum(jnp.abs(den), jnp.exp(-m_row2))
        hs_ref[pl.ds(r0, n), ML_DV * a:ML_DV * (a + 1)] = ht2[:, :n].T
        hs_ref[pl.ds(r0, n), ML_DV * b:ML_DV * (b + 1)] = ht2[:, n:].T


def _store_values_ext(vt_all, vt_ref):
    nchunk = vt_all.shape[1] // CHUNK
    sub = lax.broadcasted_iota(jnp.int32, (ML_EXT - ML_DV, 2 * CHUNK), 0)
    tail = jnp.where(sub == 0, 1.0, 0.0)
    for p in range(ML_HEADS // 2):
        for c in range(nchunk):
            tok = slice(CHUNK * c, CHUNK * (c + 1))
            for hh in range(2):
                h = 2 * p + hh
                vt_ref[p, c, 0:ML_DV, CHUNK * hh:CHUNK * (hh + 1)] = vt_all[ML_DV * h:ML_DV * (h + 1), tok]
            vt_ref[p, c, ML_DV:ML_EXT, :] = tail


def _meta_kernel(xm_ref, gpre_ref, wk_ref, wv_ref, wmqk_ref, wmvt_ref,
                 convw_ref, convb_ref, gbias_ref, rc_ref, rs1_ref, rs2_ref,
                 kmeta_ref, vmeta_ref, tail_ref, c0_ref, m0_ref,
                 cs_s, q_s, k_s, vt_s, gr_s, br_s, cb_s):
    rows = xm_ref.shape[0]
    u = _rmsnorm(xm_ref[...], gpre_ref[...]).astype(BF16)
    valid = lax.broadcasted_iota(jnp.int32, (rows, 1), 0) >= rows - N_META
    valid_t = lax.broadcasted_iota(jnp.int32, (1, rows), 1) >= rows - N_META

    kc = _dot(u, wk_ref[...])
    for cb in range(kc.shape[1] // LANES):
        sl = slice(LANES * cb, LANES * (cb + 1))
        kmeta_ref[:, sl] = _rope(kc[:, sl], rc_ref[...], rs1_ref[...], rs2_ref[...])
    vmeta_ref[...] = _dot(u, wv_ref[...])

    cs_s[0:CONV_HDR, :] = jnp.zeros((CONV_HDR, D_MODEL), F32)
    cs_s[CONV_HDR:, :] = _dot(u, wmqk_ref[...])
    tail_ref[...] = cs_s[rows:rows + CONV_HDR, :]
    _conv_qk(cs_s, convw_ref, convb_ref, q_s, k_s, 0, rows, valid)

    vt_all = _dot_nt(wmvt_ref[...], u)
    _store_values_ext(jnp.where(valid_t, vt_all, 0.0), vt_s)
    _store_gate_forms(0, _gate_rows(vt_all[ML_HEADS * ML_DV:, :], gbias_ref, valid_t), gr_s, br_s, cb_s)

    c0_ref[...] = jnp.zeros(c0_ref.shape, F32)
    m0_ref[...] = jnp.zeros(m0_ref.shape, F32)
    _mlstm_chunk_all_heads(0, gr_s, br_s, cb_s, q_s, k_s, vt_s, c0_ref, m0_ref, None)


def _attn_kernel(x_ref, gpre_ref, wqt_ref, wk_ref, wvt_ref, wz_ref, wga_ref, wao_ref,
                 rc_ref, rs1_ref, rs2_ref, rct_ref, rst_ref, km2_ref, vmt_ref, bt_ref, bt0_ref, sink_ref,
                 out_ref,
                 u_s, q_s, kk_s, vt_s, a_s, gated_s, zg_s, ga_s):
    j = pl.program_id(1)
    tile = x_ref.shape[1]
    nblk = tile // ATTN_BLOCK
    pairs = N_HEADS // 2

    @pl.when(j == 0)
    def _():
        kk_s[:, 0:ATTN_BLOCK, :] = jnp.zeros((N_KV, ATTN_BLOCK, LANES), BF16)
        vt_s[:, 0] = jnp.zeros((N_KV, HEAD_DIM, ATTN_BLOCK), BF16)

    u_s[...] = _rmsnorm(x_ref[0], gpre_ref[...]).astype(BF16)

    half_r = ROPE_DIM // 2
    cos_t, sin_t = rct_ref[...], rst_ref[...]
    zero_blk = jnp.zeros((HEAD_DIM, ATTN_BLOCK), BF16)
    qt_all = _dot_nt(wqt_ref[...], u_s[...])
    for c in range(pairs):
        qt = qt_all[2 * HEAD_DIM * c:2 * HEAD_DIM * (c + 1), :]
        parts_t = []
        for hh in range(2):
            base = HEAD_DIM * hh
            x1, x2 = qt[base:base + half_r], qt[base + half_r:base + ROPE_DIM]
            parts_t += [x1 * cos_t - x2 * sin_t, x2 * cos_t + x1 * sin_t, qt[base + ROPE_DIM:base + HEAD_DIM]]
        qb = (jnp.concatenate(parts_t, axis=0) * (HEAD_DIM ** -0.5)).astype(BF16)
        for i in range(nblk):
            blk = qb[:, ATTN_BLOCK * i:ATTN_BLOCK * (i + 1)]
            q_s[c, i] = jnp.concatenate(
                [jnp.concatenate([blk[:HEAD_DIM], zero_blk], axis=1),
                 jnp.concatenate([zero_blk, blk[HEAD_DIM:]], axis=1)], axis=0)

    c_t, s1, s2 = rc_ref[...], rs1_ref[...], rs2_ref[...]
    low = lax.broadcasted_iota(jnp.int32, (tile, LANES), 1) < HEAD_DIM

    kc = _dot(u_s[...], wk_ref[...])
    for cb in range(N_KV // 2):
        blk = _rope(kc[:, LANES * cb:LANES * (cb + 1)], c_t, s1, s2)
        swapped = pltpu.roll(blk, HEAD_DIM, 1)
        kk_s[2 * cb, ATTN_BLOCK:, :] = jnp.where(low, blk, swapped).astype(BF16)
        kk_s[2 * cb + 1, ATTN_BLOCK:, :] = jnp.where(low, swapped, blk).astype(BF16)
    vt_all = _dot_nt(wvt_ref[...], u_s[...])
    for g in range(N_KV):
        for bi in range(nblk):
            vt_s[g, 1 + bi] = vt_all[HEAD_DIM * g:HEAD_DIM * (g + 1),
                                     ATTN_BLOCK * bi:ATTN_BLOCK * (bi + 1)].astype(BF16)

    def block_scores(i):
        r0 = i * ATTN_BLOCK
        out = []
        for c in range(pairs):
            g = c // 2
            band = jnp.concatenate([kk_s[g, r0:r0 + 2 * ATTN_BLOCK, :], km2_ref[g]], axis=0)
            out.append(_dot(band, q_s[c, i]))
        return out

    def gate_slice(i):
        for r in range(tile // ROW_GROUP):
            rows = slice(ROW_GROUP * r, ROW_GROUP * (r + 1))
            zg_s[i, rows, :] = _silu(_dot(u_s[rows, :], wz_ref[i]))
            ga_s[i, rows, :] = _sigmoid(_dot(u_s[rows, :], wga_ref[i]))

    gate_slice(0)
    scores = block_scores(0)
    for i in range(nblk):
        r0 = i * ATTN_BLOCK
        bias = jnp.where(j == 0, bt0_ref[...], bt_ref[...]) if i == 0 else bt_ref[...]
        next_scores = block_scores(i + 1) if i + 1 < nblk else None
        if i + 1 < nblk:
            gate_slice(i + 1)
        probs = []
        for c in range(pairs):
            st = scores[c] + bias
            sink = sink_ref[c, 0:1, :]
            mx = jnp.maximum(jnp.max(st, axis=0, keepdims=True), sink)
            p = jnp.exp(st - mx)
            den = jnp.sum(p, axis=0, keepdims=True) + jnp.exp(sink - mx)
            probs.append((p.astype(BF16), 1.0 / den))
        for c in range(pairs):
            g = c // 2
            pb, rinv = probs[c]
            vt = jnp.concatenate([vt_s[g, i], vt_s[g, i + 1], vmt_ref[g][:, 0:N_META]], axis=1)
            ot = _dot(vt, pb) * rinv
            o = jnp.concatenate([ot[:, :ATTN_BLOCK], ot[:, ATTN_BLOCK:]], axis=0).T
            a_s[c, r0:r0 + ATTN_BLOCK, :] = o
        scores = next_scores

    kk_s[:, 0:ATTN_BLOCK, :] = kk_s[:, tile:tile + ATTN_BLOCK, :]
    vt_s[:, 0] = vt_s[:, nblk]

    width = zg_s.shape[2]
    per = width // LANES
    for g4 in range(nblk):
        att = jnp.concatenate([a_s[per * g4 + p] for p in range(per)], axis=1)
        gated_s[:, width * g4:width * (g4 + 1)] = (att * zg_s[g4]).astype(BF16)
    for g4 in range(nblk):
        cols = slice(width * g4, width * (g4 + 1))
        out_ref[0, :, cols] = (ga_s[g4] * _dot(gated_s[...], wao_ref[:, cols])).astype(out_ref.dtype)


def _mlstm_kernel(x_ref, ya_ref, gpre_ref, gpost_ref, wmqk_ref, wmvt_ref, wmo_ref, wmz_ref,
                  wgm_ref, wmlo_ref, wout_ref, convw_ref, convb_ref, gbias_ref, hnorm_ref,
                  tail0_ref, c0_ref, m0_ref,
                  out_ref,
                  u_s, cs_s, q_s, k_s, vt_s, gr_s, br_s, cb_s, hs_s, hm_s, mg_s, c_s, m_s,
                  og_s, mz_s, gm_s):
    j = pl.program_id(1)
    tile = x_ref.shape[1]

    @pl.when(j == 0)
    def _():
        cs_s[0:CONV_HDR, :] = tail0_ref[...]
        c_s[...] = c0_ref[...]
        m_s[...] = m0_ref[...]

    u_s[...] = _rmsnorm(x_ref[0], gpre_ref[...]).astype(BF16)

    vt_all = _dot_nt(wmvt_ref[...], u_s[...])
    for c in range(tile // CHUNK):
        graw = vt_all[ML_HEADS * ML_DV:, CHUNK * c:CHUNK * (c + 1)]
        _store_gate_forms(c, _gate_rows(graw, gbias_ref), gr_s, br_s, cb_s)
    _store_values_ext(vt_all, vt_s)
    for half in range(2):
        cols = slice(512 * half, 512 * (half + 1))
        cs_s[CONV_HDR:, cols] = _dot(u_s[...], wmqk_ref[:, cols])

    for c in range(tile // CHUNK):
        def gate_slice():
            wcat = jnp.concatenate([wmo_ref[c], wmz_ref[c], wgm_ref[c]], axis=1)
            gates = _dot(u_s[...], wcat)
            og_s[c] = _sigmoid(gates[:, 0:ML_DV])
            mz_s[c] = _silu(gates[:, ML_DV:2 * ML_DV])
            gm_s[c] = _sigmoid(gates[:, 2 * ML_DV:])

        _conv_qk(cs_s, convw_ref, convb_ref, q_s, k_s, CHUNK * c, CHUNK)
        _mlstm_chunk_all_heads(c, gr_s, br_s, cb_s, q_s, k_s, vt_s, c_s, m_s, hs_s, gate_slice)
    cs_s[0:CONV_HDR, :] = cs_s[tile:tile + CONV_HDR, :]

    for r in range(tile // ROW_GROUP):
        rows = slice(ROW_GROUP * r, ROW_GROUP * (r + 1))
        for h in range(ML_HEADS):
            cols = slice(ML_DV * h, ML_DV * (h + 1))
            hg = og_s[h, rows, :] * hs_s[rows, cols]
            hc = hg - jnp.mean(hg, axis=-1, keepdims=True)
            y = hc * lax.rsqrt(jnp.mean(hc * hc, axis=-1, keepdims=True) + LN_EPS) * hnorm_ref[:, cols]
            hm_s[rows, cols] = (y * mz_s[h, rows, :]).astype(BF16)
        for h in range(ML_HEADS):
            cols = slice(ML_DV * h, ML_DV * (h + 1))
            ym = _dot(hm_s[rows, :], wmlo_ref[:, cols])
            mg_s[rows, cols] = (ya_ref[0, rows, cols].astype(F32) + gm_s[h, rows, :] * ym).astype(BF16)
        o = _dot(mg_s[rows, :], wout_ref[...])
        out_ref[0, rows, :] = x_ref[0, rows, :] + _rmsnorm(o, gpost_ref[...])


def _rope_tables(pos):
    half = ROPE_DIM // 2
    inv_freq = ROPE_THETA ** (-jnp.arange(0, ROPE_DIM, 2, dtype=F32) / ROPE_DIM)
    ang = pos[:, None] * inv_freq[None, :]
    cos, sin = jnp.cos(ang), jnp.sin(ang)
    ones = jnp.ones((pos.shape[0], HEAD_DIM - ROPE_DIM), F32)
    zeros_h = jnp.zeros((pos.shape[0], half), F32)
    zeros_r = jnp.zeros_like(ones)
    c_head = jnp.concatenate([cos, cos, ones], axis=1)
    s1_head = jnp.concatenate([-sin, zeros_h, zeros_r], axis=1)
    s2_head = jnp.concatenate([zeros_h, sin, zeros_r], axis=1)
    rep = LANES // HEAD_DIM
    return (jnp.tile(c_head, (1, rep)), jnp.tile(s1_head, (1, rep)), jnp.tile(s2_head, (1, rep)))


def _band_bias():
    kj = np.arange(2 * ATTN_BLOCK)[:, None]
    qi = np.arange(ATTN_BLOCK)[None, :]
    vis = (kj > qi) & (kj <= qi + ATTN_BLOCK)
    meta = np.zeros((N_META, ATTN_BLOCK), np.float32)
    bt = np.concatenate([np.where(vis, 0.0, NEG_BIG), meta], axis=0).astype(np.float32)
    bt0 = np.concatenate([np.where(vis & (kj >= ATTN_BLOCK), 0.0, NEG_BIG), meta], axis=0).astype(np.float32)
    return jnp.asarray(np.tile(bt, (1, 2))), jnp.asarray(np.tile(bt0, (1, 2)))


def _full(shape, single=True):
    nd = len(shape)
    kw = {"pipeline_mode": pl.Buffered(1)} if single else {}
    return pl.BlockSpec(shape, lambda *_: (0,) * nd, **kw)


def kernel(x, meta_tokens, norm_pre, w_in, attn_sinks, conv_w, conv_b, mlstm_gate_bias, mlstm_head_norm,
           w_attn_out, w_mlstm_out, w_out, norm_post):
    B, S, D = x.shape
    assert D == D_MODEL and w_in.shape[0] == 1 and meta_tokens.shape == (N_META, D_MODEL)
    tile = min(SEQ_TILE, S)
    assert S % tile == 0 and tile % CHUNK == 0 and tile % ATTN_BLOCK == 0
    nt = S // tile
    nblk = tile // ATTN_BLOCK
    nchunk = tile // CHUNK

    w = w_in[0].astype(BF16)
    o = 0
    parts = {}
    for name, width in (("q", 1024), ("k", 256), ("v", 256), ("z", 1024), ("mqk", 1024), ("mv", 1024),
                        ("mi", 4), ("mf", 4), ("mo", 1024), ("mz", 1024), ("ga", 1024), ("gm", 1024)):
        parts[name] = w[:, o:o + width]
        o += width
    nslice = tile // ATTN_BLOCK
    assert nslice == nchunk == ML_HEADS and D // nslice == ML_DV

    def sliced(wm):
        return wm.reshape(D, nslice, wm.shape[1] // nslice).transpose(1, 0, 2)

    wvt = parts["v"].T
    wqt = parts["q"].T
    gate_rows = 16
    wmvt = jnp.concatenate([parts["mv"].T, parts["mi"].T, parts["mf"].T,
                            jnp.zeros((gate_rows - 2 * ML_HEADS, D), BF16)], axis=0)
    gbias = jnp.broadcast_to(jnp.concatenate([mlstm_gate_bias[0].astype(F32),
                                              jnp.zeros((gate_rows - 2 * ML_HEADS,), F32)])[:, None],
                             (gate_rows, CHUNK))
    gpre = norm_pre[0].astype(F32)[None, :]
    gpost = norm_post[0].astype(F32)[None, :]
    convw = conv_w[0].astype(F32)
    convb = conv_b[0].astype(F32)[None, :]
    hnorm = mlstm_head_norm[0].astype(F32)[None, :]
    wao = w_attn_out[0].astype(BF16)
    wmlo = w_mlstm_out[0].astype(BF16)
    wout = w_out[0].astype(BF16)
    sinks = attn_sinks[0].astype(F32)
    sink_rows = jnp.broadcast_to(jnp.repeat(sinks.reshape(N_HEADS // 2, 2), ATTN_BLOCK, axis=1)[:, None, :],
                                 (N_HEADS // 2, 8, 2 * ATTN_BLOCK))

    xm = jnp.concatenate([jnp.zeros((CHUNK - N_META, D), F32), meta_tokens.astype(F32)], axis=0)
    mpos = jnp.maximum(jnp.arange(CHUNK, dtype=F32) - (CHUNK - N_META), 0.0)
    mrc, mrs1, mrs2 = _rope_tables(mpos)
    kv_w = N_KV * HEAD_DIM
    meta_out = pl.pallas_call(
        _meta_kernel,
        out_shape=(jax.ShapeDtypeStruct((CHUNK, kv_w), F32), jax.ShapeDtypeStruct((CHUNK, kv_w), F32),
                   jax.ShapeDtypeStruct((CONV_HDR, D), F32),
                   jax.ShapeDtypeStruct((ML_HEADS // 2, ML_EXT, 2 * ML_DK), F32),
                   jax.ShapeDtypeStruct((8, LANES), F32)),
        scratch_shapes=[pltpu.VMEM((CONV_HDR + CHUNK, D), F32), pltpu.VMEM((CHUNK, D // 2), BF16),
                        pltpu.VMEM((CHUNK, D // 2), BF16), pltpu.VMEM((ML_HEADS // 2, 1, ML_EXT, 2 * CHUNK), F32),
                        pltpu.VMEM((1, 8, CHUNK), F32), pltpu.VMEM((1, 8, CHUNK), F32),
                        pltpu.VMEM((1, CHUNK, LANES), F32)],
        compiler_params=pltpu.CompilerParams(vmem_limit_bytes=VMEM_LIMIT),
        name="meta_tokens",
    )(xm, gpre, parts["k"], parts["v"], parts["mqk"], wmvt, convw, convb, gbias,
      mrc, mrs1, mrs2)
    kmeta, vmeta, tail0, c0, m0 = meta_out
    km = kmeta[CHUNK - N_META:].reshape(N_META, N_KV, HEAD_DIM).transpose(1, 0, 2)
    km2 = jnp.concatenate([km, km], axis=2).astype(BF16)
    vm = vmeta[CHUNK - N_META:].reshape(N_META, N_KV, HEAD_DIM).transpose(1, 2, 0)
    vmt = jnp.concatenate([vm, jnp.zeros((N_KV, HEAD_DIM, LANES - N_META), F32)], axis=2).astype(BF16)

    rc, rs1, rs2 = _rope_tables(jnp.arange(S, dtype=F32) + N_META)
    rct, rst = rc[:, 0:ROPE_DIM // 2].T, rs2[:, ROPE_DIM // 2:ROPE_DIM].T
    bt, bt0 = _band_bias()
    x_spec = pl.BlockSpec((1, tile, D), lambda b, j: (b, j, 0))
    rope_spec = pl.BlockSpec((tile, LANES), lambda b, j: (j, 0))
    rope_t_spec = pl.BlockSpec((ROPE_DIM // 2, tile), lambda b, j: (0, j))
    seq_params = pltpu.CompilerParams(dimension_semantics=("arbitrary", "arbitrary"),
                                      vmem_limit_bytes=VMEM_LIMIT)
    ya = pl.pallas_call(
        _attn_kernel,
        grid=(B, nt),
        in_specs=[x_spec, _full((1, D)),
                  _full((D, 1024)), _full((D, kv_w)), _full((kv_w, D)), _full((nslice, D, ML_DV)),
                  _full((nslice, D, ML_DV)), _full((D, D)),
                  rope_spec, rope_spec, rope_spec, rope_t_spec, rope_t_spec,
                  _full((N_KV, N_META, LANES)), _full((N_KV, HEAD_DIM, LANES)),
                  _full((BAND, 2 * ATTN_BLOCK)), _full((BAND, 2 * ATTN_BLOCK)),
                  _full((N_HEADS // 2, 8, 2 * ATTN_BLOCK))],
        out_specs=pl.BlockSpec((1, tile, D), lambda b, j: (b, j, 0)),
        out_shape=jax.ShapeDtypeStruct((B, S, D), BF16),
        scratch_shapes=[pltpu.VMEM((tile, D), BF16),
                        pltpu.VMEM((N_HEADS // 2, nblk, 2 * HEAD_DIM, 2 * ATTN_BLOCK), BF16),
                        pltpu.VMEM((N_KV, ATTN_BLOCK + tile, LANES), BF16),
                        pltpu.VMEM((N_KV, nblk + 1, HEAD_DIM, ATTN_BLOCK), BF16),
                        pltpu.VMEM((N_HEADS // 2, tile, LANES), F32),
                        pltpu.VMEM((tile, D), BF16),
                        pltpu.VMEM((nslice, tile, ML_DV), F32),
                        pltpu.VMEM((nslice, tile, ML_DV), F32)],
        compiler_params=seq_params,
        name="attn_branch",
    )(x, gpre, wqt, parts["k"], wvt, sliced(parts["z"]), sliced(parts["ga"]), wao,
      rc, rs1, rs2, rct, rst, km2, vmt, bt, bt0, sink_rows)

    out = pl.pallas_call(
        _mlstm_kernel,
        grid=(B, nt),
        in_specs=[x_spec, pl.BlockSpec((1, tile, D), lambda b, j: (b, j, 0)),
                  _full((1, D)), _full((1, D)),
                  _full((D, 1024)), _full((1024 + gate_rows, D)), _full((nslice, D, ML_DV)),
                  _full((nslice, D, ML_DV)), _full((nslice, D, ML_DV)), _full((D, D)), _full((D, D)),
                  _full((CONV_W, D)), _full((1, D)), _full((gate_rows, CHUNK)), _full((1, D)),
                  _full((CONV_HDR, D)), _full((ML_HEADS // 2, ML_EXT, 2 * ML_DK)), _full((8, LANES))],
        out_specs=pl.BlockSpec((1, tile, D), lambda b, j: (b, j, 0)),
        out_shape=jax.ShapeDtypeStruct((B, S, D), x.dtype),
        scratch_shapes=[pltpu.VMEM((tile, D), BF16),
                        pltpu.VMEM((CONV_HDR + tile, D), F32),
                        pltpu.VMEM((tile, D // 2), BF16),
                        pltpu.VMEM((tile, D // 2), BF16),
                        pltpu.VMEM((ML_HEADS // 2, nchunk, ML_EXT, 2 * CHUNK), F32),
                        pltpu.VMEM((nchunk, 8, CHUNK), F32),
                        pltpu.VMEM((nchunk, 8, CHUNK), F32),
                        pltpu.VMEM((nchunk, CHUNK, LANES), F32),
                        pltpu.VMEM((tile, D), F32),
                        pltpu.VMEM((tile, D), BF16),
                        pltpu.VMEM((tile, D), BF16),
                        pltpu.VMEM((ML_HEADS // 2, ML_EXT, 2 * ML_DK), F32),
                        pltpu.VMEM((8, LANES), F32),
                        pltpu.VMEM((nslice, tile, ML_DV), F32),
                        pltpu.VMEM((nslice, tile, ML_DV), F32),
                        pltpu.VMEM((nslice, tile, ML_DV), F32)],
        compiler_params=seq_params,
        name="mlstm_merge_out",
    )(x, ya, gpre, gpost, parts["mqk"], wmvt, sliced(parts["mo"]), sliced(parts["mz"]),
      sliced(parts["gm"]),
      wmlo, wout, convw, convb, gbias, hnorm, tail0, c0, m0)
    return out
```

```python
import numpy as np
import jax
import jax.numpy as jnp
from jax import lax
from jax.experimental import pallas as pl
from jax.experimental.pallas import tpu as pltpu

F32 = jnp.float32
BF16 = jnp.bfloat16

D_MODEL = 1024
N_META = 16
HEAD_DIM = 64
N_HEADS = 16
N_KV = 4
ATTN_BLOCK = 128
BAND = N_META + 2 * ATTN_BLOCK
ROPE_DIM = 16
ROPE_THETA = 500000.0
ML_HEADS = 4
ML_DV = 256
ML_DK = 128
ML_EXT = ML_DV + 16
CONV_W = 4
CHUNK = 128
RMS_EPS = 1e-6
LN_EPS = 1e-6
NEG_BIG = -1e30
LANES = 128
CONV_HDR = 8
SEQ_TILE = 512
ROW_GROUP = 256
VMEM_LIMIT = 56 * 1024 * 1024


def _dot(a, b):
    return jnp.dot(a, b, preferred_element_type=F32)


def _dot_nt(a, b):
    return lax.dot_general(a, b, (((1,), (1,)), ((), ())), preferred_element_type=F32)


def _sigmoid(x):
    return 0.5 * jnp.tanh(0.5 * x) + 0.5


def _silu(x):
    return x * _sigmoid(x)


def _log_sigmoid(x):
    return jnp.minimum(x, 0.0) - jnp.log(1.0 + jnp.exp(-jnp.abs(x)))


def _rmsnorm(xf, g):
    ms = jnp.mean(xf * xf, axis=-1, keepdims=True)
    return xf * lax.rsqrt(ms + RMS_EPS) * g


def _rope(blk, c_t, s1, s2):
    return blk * c_t + pltpu.roll(blk, LANES - 8, 1) * s1 + pltpu.roll(blk, 8, 1) * s2


def _causal_conv_silu(cs_ref, w_ref, b_ref, row0, rows, cols):
    acc = b_ref[:, cols]
    for tap in range(CONV_W):
        off = CONV_HDR - (CONV_W - 1) + tap + row0
        acc = acc + w_ref[tap:tap + 1, cols] * cs_ref[off:off + rows, cols]
    return _silu(acc)


def _conv_qk(cs_ref, w_ref, b_ref, q_ref, k_ref, row0, rows, valid=None):
    half = D_MODEL // 2
    for cc in range(4):
        cols = slice(256 * cc, 256 * (cc + 1))
        act = _causal_conv_silu(cs_ref, w_ref, b_ref, row0, rows, cols)
        if valid is not None:
            act = jnp.where(valid, act, 0.0)
        if cc < 2:
            q_ref[row0:row0 + rows, cols] = act.astype(BF16)
        else:
            k_ref[row0:row0 + rows, 256 * cc - half:256 * (cc + 1) - half] = (
                act * (ML_DK ** -0.5)).astype(BF16)


def _gate_rows(graw, gbias_ref, valid_t=None):
    g = (graw + gbias_ref[...])[0:8, :]
    sub = lax.broadcasted_iota(jnp.int32, g.shape, 0)
    grow = jnp.where(sub < ML_HEADS, g, _log_sigmoid(g))
    if valid_t is not None:
        grow = jnp.where(valid_t, grow, jnp.where(sub < ML_HEADS, NEG_BIG, 0.0))
    return grow


def _store_gate_forms(c, grow, gr_ref, br_ref, cb_ref):
    lane = lax.broadcasted_iota(jnp.int32, grow.shape, 1)
    brow = grow
    sh = 1
    while sh < CHUNK:
        brow = brow + jnp.where(lane >= sh, pltpu.roll(brow, sh, 1), 0.0)
        sh *= 2
    diff = brow[ML_HEADS:2 * ML_HEADS, :] - grow[0:ML_HEADS, :]
    pad = jnp.zeros((LANES - ML_HEADS, CHUNK), F32)
    gr_ref[c], br_ref[c], cb_ref[c] = grow, brow, jnp.concatenate([diff, pad], axis=0).T


def _mlstm_chunk_all_heads(c, gr_ref, br_ref, cb_ref, q_ref, k_ref, vt_ref, c_ref, m_ref, hs_ref,
                           between=None):
    r0 = c * CHUNK if isinstance(c, int) else pl.multiple_of(c * CHUNK, CHUNK)
    n = CHUNK
    grow, brow, cbm = gr_ref[c], br_ref[c], cb_ref[c]
    row = lax.broadcasted_iota(jnp.int32, (n, 2 * n), 0)
    col = lax.broadcasted_iota(jnp.int32, (n, 2 * n), 1)
    tri2 = row <= (col & (n - 1))
    lo_row = lax.broadcasted_iota(jnp.int32, (1, 2 * n), 1) < n
    lo = lax.broadcasted_iota(jnp.int32, (n, 2 * ML_DK), 1) < ML_DK

    def block_diag(x2):
        zero = jnp.zeros_like(x2)
        return jnp.concatenate([jnp.where(lo, x2, zero), jnp.where(lo, zero, x2)], axis=0)

    staged = []
    for p in range(ML_HEADS // 2):
        a, b = 2 * p, 2 * p + 1
        cols = slice(2 * ML_DK * p, 2 * ML_DK * (p + 1))
        q2, k2, vt2, c2 = q_ref[pl.ds(r0, n), cols], k_ref[pl.ds(r0, n), cols], vt_ref[p, c], c_ref[p]
        li2 = jnp.concatenate([grow[a:a + 1, :], grow[b:b + 1, :]], axis=1)
        b2 = jnp.concatenate([brow[ML_HEADS + a:ML_HEADS + a + 1, :],
                              brow[ML_HEADS + b:ML_HEADS + b + 1, :]], axis=1)
        m_a, m_b = m_ref[a:a + 1, 0:1], m_ref[b:b + 1, 0:1]
        m_prev2 = jnp.where(lo_row, m_a, m_b)
        bl_a, bl_b = b2[:, n - 1:n], b2[:, 2 * n - 1:2 * n]
        b_last2 = jnp.where(lo_row, bl_a, bl_b)
        sc = _dot_nt(jnp.concatenate([k2, c2.astype(BF16)], axis=0), block_diag(q2))
        a_row2 = b_last2 - b2 + li2
        mn_a = jnp.maximum(bl_a + m_a, jnp.max(a_row2[:, :n], axis=1, keepdims=True))
        mn_b = jnp.maximum(bl_b + m_b, jnp.max(a_row2[:, n:], axis=1, keepdims=True))
        m_new2 = jnp.where(lo_row, mn_a, mn_b)
        w_c2 = jnp.exp(b_last2 + m_prev2 - m_new2)
        w_a2 = jnp.exp(a_row2 - m_new2)
        c_ref[p] = w_c2 * c2 + _dot((vt2 * w_a2).astype(BF16), block_diag(k2))
        m_ref[a:a + 1, :] = jnp.broadcast_to(mn_a, (1, LANES))
        m_ref[b:b + 1, :] = jnp.broadcast_to(mn_b, (1, LANES))
        staged.append((sc, vt2, b2, m_prev2))
    if between is not None:
        between()
    if hs_ref is None:
        return
    for p, (sc, vt2, b2, m_prev2) in enumerate(staged):
        a, b = 2 * p, 2 * p + 1
        cb2 = jnp.concatenate([jnp.broadcast_to(cbm[:, a:a + 1], (n, n)),
                               jnp.broadcast_to(cbm[:, b:b + 1], (n, n))], axis=1)
        dt = jnp.where(tri2, b2 - cb2, NEG_BIG)
        inter2 = b2 + m_prev2
        m_row2 = jnp.maximum(inter2, jnp.max(dt, axis=0, keepdims=True))
        sct2 = (sc[0:n] * jnp.exp(dt - m_row2)).astype(BF16)
        het = jnp.exp(inter2 - m_row2) * sc[n:] + _dot(vt2.astype(BF16), block_diag(sct2))
        den = het[ML_DV:ML_DV + 1, :]
        ht2 = het[0:ML_DV, :] / jnp.maximum(jnp.abs(den), jnp.exp(-m_row2))
        hs_ref[pl.ds(r0, n), ML_DV * a:ML_DV * (a + 1)] = ht2[:, :n].T
        hs_ref[pl.ds(r0, n), ML_DV * b:ML_DV * (b + 1)] = ht2[:, n:].T


def _store_values_ext(vt_all, vt_ref):
    nchunk = vt_all.shape[1] // CHUNK
    sub = lax.broadcasted_iota(jnp.int32, (ML_EXT - ML_DV, 2 * CHUNK), 0)
    tail = jnp.where(sub == 0, 1.0, 0.0)
    for p in range(ML_HEADS // 2):
        for c in range(nchunk):
            tok = slice(CHUNK * c, CHUNK * (c + 1))
            for hh in range(2):
                h = 2 * p + hh
                vt_ref[p, c, 0:ML_DV, CHUNK * hh:CHUNK * (hh + 1)] = vt_all[ML_DV * h:ML_DV * (h + 1), tok]
            vt_ref[p, c, ML_DV:ML_EXT, :] = tail


def _meta_kernel(xm_ref, gpre_ref, wk_ref, wv_ref, wmqk_ref, wmvt_ref,
                 convw_ref, convb_ref, gbias_ref, rc_ref, rs1_ref, rs2_ref,
                 kmeta_ref, vmeta_ref, tail_ref, c0_ref, m0_ref,
                 cs_s, q_s, k_s, vt_s, gr_s, br_s, cb_s):
    rows = xm_ref.shape[0]
    u = _rmsnorm(xm_ref[...], gpre_ref[...]).astype(BF16)
    valid = lax.broadcasted_iota(jnp.int32, (rows, 1), 0) >= rows - N_META
    valid_t = lax.broadcasted_iota(jnp.int32, (1, rows), 1) >= rows - N_META

    kc = _dot(u, wk_ref[...])
    for cb in range(kc.shape[1] // LANES):
        sl = slice(LANES * cb, LANES * (cb + 1))
        kmeta_ref[:, sl] = _rope(kc[:, sl], rc_ref[...], rs1_ref[...], rs2_ref[...])
    vmeta_ref[...] = _dot(u, wv_ref[...])

    cs_s[0:CONV_HDR, :] = jnp.zeros((CONV_HDR, D_MODEL), F32)
    cs_s[CONV_HDR:, :] = _dot(u, wmqk_ref[...])
    tail_ref[...] = cs_s[rows:rows + CONV_HDR, :]
    _conv_qk(cs_s, convw_ref, convb_ref, q_s, k_s, 0, rows, valid)

    vt_all = _dot_nt(wmvt_ref[...], u)
    _store_values_ext(jnp.where(valid_t, vt_all, 0.0), vt_s)
    _store_gate_forms(0, _gate_rows(vt_all[ML_HEADS * ML_DV:, :], gbias_ref, valid_t), gr_s, br_s, cb_s)

    c0_ref[...] = jnp.zeros(c0_ref.shape, F32)
    m0_ref[...] = jnp.zeros(m0_ref.shape, F32)
    _mlstm_chunk_all_heads(0, gr_s, br_s, cb_s, q_s, k_s, vt_s, c0_ref, m0_ref, None)


def _attn_kernel(x_ref, gpre_ref, wqt_ref, wk_ref, wvt_ref, wz_ref, wga_ref, wao_ref,
                 rc_ref, rs1_ref, rs2_ref, rct_ref, rst_ref, km2_ref, vmt_ref, bt_ref, bt0_ref, sink_ref,
                 out_ref,
                 u_s, q_s, kk_s, vt_s, a_s, gated_s, zg_s, ga_s):
    j = pl.program_id(1)
    tile = x_ref.shape[1]
    nblk = tile // ATTN_BLOCK
    pairs = N_HEADS // 2

    @pl.when(j == 0)
    def _():
        kk_s[:, 0:ATTN_BLOCK, :] = jnp.zeros((N_KV, ATTN_BLOCK, LANES), BF16)
        vt_s[:, 0] = jnp.zeros((N_KV, HEAD_DIM, ATTN_BLOCK), BF16)

    u_s[...] = _rmsnorm(x_ref[0], gpre_ref[...]).astype(BF16)

    half_r = ROPE_DIM // 2
    cos_t, sin_t = rct_ref[...], rst_ref[...]
    zero_blk = jnp.zeros((HEAD_DIM, ATTN_BLOCK), BF16)
    qt_all = _dot_nt(wqt_ref[...], u_s[...])
    for c in range(pairs):
        qt = qt_all[2 * HEAD_DIM * c:2 * HEAD_DIM * (c + 1), :]
        parts_t = []
        for hh in range(2):
            base = HEAD_DIM * hh
            x1, x2 = qt[base:base + half_r], qt[base + half_r:base + ROPE_DIM]
            parts_t += [x1 * cos_t - x2 * sin_t, x2 * cos_t + x1 * sin_t, qt[base + ROPE_DIM:base + HEAD_DIM]]
        qb = (jnp.concatenate(parts_t, axis=0) * (HEAD_DIM ** -0.5)).astype(BF16)
        for i in range(nblk):
            blk = qb[:, ATTN_BLOCK * i:ATTN_BLOCK * (i + 1)]
            q_s[c, i] = jnp.concatenate(
                [jnp.concatenate([blk[:HEAD_DIM], zero_blk], axis=1),
                 jnp.concatenate([zero_blk, blk[HEAD_DIM:]], axis=1)], axis=0)

    c_t, s1, s2 = rc_ref[...], rs1_ref[...], rs2_ref[...]
    low = lax.broadcasted_iota(jnp.int32, (tile, LANES), 1) < HEAD_DIM

    kc = _dot(u_s[...], wk_ref[...])
    for cb in range(N_KV // 2):
        blk = _rope(kc[:, LANES * cb:LANES * (cb + 1)], c_t, s1, s2)
        swapped = pltpu.roll(blk, HEAD_DIM, 1)
        kk_s[2 * cb, ATTN_BLOCK:, :] = jnp.where(low, blk, swapped).astype(BF16)
        kk_s[2 * cb + 1, ATTN_BLOCK:, :] = jnp.where(low, swapped, blk).astype(BF16)
    vt_all = _dot_nt(wvt_ref[...], u_s[...])
    for g in range(N_KV):
        for bi in range(nblk):
            vt_s[g, 1 + bi] = vt_all[HEAD_DIM * g:HEAD_DIM * (g + 1),
                                     ATTN_BLOCK * bi:ATTN_BLOCK * (bi + 1)].astype(BF16)

    def block_scores(i):
        r0 = i * ATTN_BLOCK
        out = []
        for c in range(pairs):
            g = c // 2
            band = jnp.concatenate([kk_s[g, r0:r0 + 2 * ATTN_BLOCK, :], km2_ref[g]], axis=0)
            out.append(_dot(band, q_s[c, i]))
        return out

    always = pl.program_id(0) >= 0

    def gate_slice(i, anchors=None):
        k = 0
        for r in range(tile // ROW_GROUP):
            rows = slice(ROW_GROUP * r, ROW_GROUP * (r + 1))
            for w_ref, dst, act in ((wz_ref, zg_s, _silu), (wga_ref, ga_s, _sigmoid)):
                lhs = u_s[rows, :]
                if anchors is not None:
                    head = jnp.where(always, lhs[0:16, 0:LANES], anchors[k])
                    lhs = jnp.concatenate(
                        [jnp.concatenate([head, lhs[0:16, LANES:]], axis=1), lhs[16:]], axis=0)
                dst[i, rows, :] = act(_dot(lhs, w_ref[i]))
                k += 1

    scores = block_scores(0)
    for i in range(nblk):
        r0 = i * ATTN_BLOCK
        bias = jnp.where(j == 0, bt0_ref[...], bt_ref[...]) if i == 0 else bt_ref[...]
        next_scores = block_scores(i + 1) if i + 1 < nblk else None
        probs = []
        for c in range(pairs):
            st = scores[c] + bias
            sink = sink_ref[c, 0:1, :]
            mx = jnp.maximum(jnp.max(st, axis=0, keepdims=True), sink)
            p = jnp.exp(st - mx)
            den = jnp.sum(p, axis=0, keepdims=True) + jnp.exp(sink - mx)
            probs.append((p.astype(BF16), 1.0 / den))
        gate_slice(i, [probs[2 * k][0][0:16, 0:LANES] for k in range(4)])
        for c in range(pairs):
            g = c // 2
            pb, rinv = probs[c]
            vt = jnp.concatenate([vt_s[g, i], vt_s[g, i + 1], vmt_ref[g][:, 0:N_META]], axis=1)
            ot = _dot(vt, pb) * rinv
            o = jnp.concatenate([ot[:, :ATTN_BLOCK], ot[:, ATTN_BLOCK:]], axis=0).T
            a_s[c, r0:r0 + ATTN_BLOCK, :] = o
        scores = next_scores

    kk_s[:, 0:ATTN_BLOCK, :] = kk_s[:, tile:tile + ATTN_BLOCK, :]
    vt_s[:, 0] = vt_s[:, nblk]

    width = zg_s.shape[2]
    per = width // LANES
    for g4 in range(nblk):
        att = jnp.concatenate([a_s[per * g4 + p] for p in range(per)], axis=1)
        gated_s[:, width * g4:width * (g4 + 1)] = (att * zg_s[g4]).astype(BF16)
    for g4 in range(nblk):
        cols = slice(width * g4, width * (g4 + 1))
        out_ref[0, :, cols] = (ga_s[g4] * _dot(gated_s[...], wao_ref[:, cols])).astype(out_ref.dtype)


def _mlstm_kernel(x_ref, ya_ref, gpre_ref, gpost_ref, wmqk_ref, wmvt_ref, wmo_ref, wmz_ref,
                  wgm_ref, wmlo_ref, wout_ref, convw_ref, convb_ref, gbias_ref, hnorm_ref,
                  tail0_ref, c0_ref, m0_ref,
                  out_ref,
                  u_s, cs_s, q_s, k_s, vt_s, gr_s, br_s, cb_s, hs_s, hm_s, mg_s, c_s, m_s,
                  og_s, mz_s, gm_s):
    j = pl.program_id(1)
    tile = x_ref.shape[1]

    @pl.when(j == 0)
    def _():
        cs_s[0:CONV_HDR, :] = tail0_ref[...]
        c_s[...] = c0_ref[...]
        m_s[...] = m0_ref[...]

    u_s[...] = _rmsnorm(x_ref[0], gpre_ref[...]).astype(BF16)

    vt_all = _dot_nt(wmvt_ref[...], u_s[...])
    for c in range(tile // CHUNK):
        graw = vt_all[ML_HEADS * ML_DV:, CHUNK * c:CHUNK * (c + 1)]
        _store_gate_forms(c, _gate_rows(graw, gbias_ref), gr_s, br_s, cb_s)
    _store_values_ext(vt_all, vt_s)
    for half in range(2):
        cols = slice(512 * half, 512 * (half + 1))
        cs_s[CONV_HDR:, cols] = _dot(u_s[...], wmqk_ref[:, cols])

    for c in range(tile // CHUNK):
        def gate_slice():
            wcat = jnp.concatenate([wmo_ref[c], wmz_ref[c], wgm_ref[c]], axis=1)
            gates = _dot(u_s[...], wcat)
            og_s[c] = _sigmoid(gates[:, 0:ML_DV])
            mz_s[c] = _silu(gates[:, ML_DV:2 * ML_DV])
            gm_s[c] = _sigmoid(gates[:, 2 * ML_DV:])

        _conv_qk(cs_s, convw_ref, convb_ref, q_s, k_s, CHUNK * c, CHUNK)
        _mlstm_chunk_all_heads(c, gr_s, br_s, cb_s, q_s, k_s, vt_s, c_s, m_s, hs_s, gate_slice)
    cs_s[0:CONV_HDR, :] = cs_s[tile:tile + CONV_HDR, :]

    for r in range(tile // ROW_GROUP):
        rows = slice(ROW_GROUP * r, ROW_GROUP * (r + 1))
        for h in range(ML_HEADS):
            cols = slice(ML_DV * h, ML_DV * (h + 1))
            hg = og_s[h, rows, :] * hs_s[rows, cols]
            hc = hg - jnp.mean(hg, axis=-1, keepdims=True)
            y = hc * lax.rsqrt(jnp.mean(hc * hc, axis=-1, keepdims=True) + LN_EPS) * hnorm_ref[:, cols]
            hm_s[rows, cols] = (y * mz_s[h, rows, :]).astype(BF16)
        for h in range(ML_HEADS):
            cols = slice(ML_DV * h, ML_DV * (h + 1))
            ym = _dot(hm_s[rows, :], wmlo_ref[:, cols])
            mg_s[rows, cols] = (ya_ref[0, rows, cols].astype(F32) + gm_s[h, rows, :] * ym).astype(BF16)
        o = _dot(mg_s[rows, :], wout_ref[...])
        out_ref[0, rows, :] = x_ref[0, rows, :] + _rmsnorm(o, gpost_ref[...])


def _rope_tables(pos):
    half = ROPE_DIM // 2
    inv_freq = ROPE_THETA ** (-jnp.arange(0, ROPE_DIM, 2, dtype=F32) / ROPE_DIM)
    ang = pos[:, None] * inv_freq[None, :]
    cos, sin = jnp.cos(ang), jnp.sin(ang)
    ones = jnp.ones((pos.shape[0], HEAD_DIM - ROPE_DIM), F32)
    zeros_h = jnp.zeros((pos.shape[0], half), F32)
    zeros_r = jnp.zeros_like(ones)
    c_head = jnp.concatenate([cos, cos, ones], axis=1)
    s1_head = jnp.concatenate([-sin, zeros_h, zeros_r], axis=1)
    s2_head = jnp.concatenate([zeros_h, sin, zeros_r], axis=1)
    rep = LANES // HEAD_DIM
    return (jnp.tile(c_head, (1, rep)), jnp.tile(s1_head, (1, rep)), jnp.tile(s2_head, (1, rep)))


def _band_bias():
    kj = np.arange(2 * ATTN_BLOCK)[:, None]
    qi = np.arange(ATTN_BLOCK)[None, :]
    vis = (kj > qi) & (kj <= qi + ATTN_BLOCK)
    meta = np.zeros((N_META, ATTN_BLOCK), np.float32)
    bt = np.concatenate([np.where(vis, 0.0, NEG_BIG), meta], axis=0).astype(np.float32)
    bt0 = np.concatenate([np.where(vis & (kj >= ATTN_BLOCK), 0.0, NEG_BIG), meta], axis=0).astype(np.float32)
    return jnp.asarray(np.tile(bt, (1, 2))), jnp.asarray(np.tile(bt0, (1, 2)))


def _full(shape, single=True):
    nd = len(shape)
    kw = {"pipeline_mode": pl.Buffered(1)} if single else {}
    return pl.BlockSpec(shape, lambda *_: (0,) * nd, **kw)


def kernel(x, meta_tokens, norm_pre, w_in, attn_sinks, conv_w, conv_b, mlstm_gate_bias, mlstm_head_norm,
           w_attn_out, w_mlstm_out, w_out, norm_post):
    B, S, D = x.shape
    assert D == D_MODEL and w_in.shape[0] == 1 and meta_tokens.shape == (N_META, D_MODEL)
    tile = min(SEQ_TILE, S)
    assert S % tile == 0 and tile % CHUNK == 0 and tile % ATTN_BLOCK == 0
    nt = S // tile
    nblk = tile // ATTN_BLOCK
    nchunk = tile // CHUNK

    w = w_in[0].astype(BF16)
    o = 0
    parts = {}
    for name, width in (("q", 1024), ("k", 256), ("v", 256), ("z", 1024), ("mqk", 1024), ("mv", 1024),
                        ("mi", 4), ("mf", 4), ("mo", 1024), ("mz", 1024), ("ga", 1024), ("gm", 1024)):
        parts[name] = w[:, o:o + width]
        o += width
    nslice = tile // ATTN_BLOCK
    assert nslice == nchunk == ML_HEADS and D // nslice == ML_DV

    def sliced(wm):
        return wm.reshape(D, nslice, wm.shape[1] // nslice).transpose(1, 0, 2)

    wvt = parts["v"].T
    wqt = parts["q"].T
    gate_rows = 16
    wmvt = jnp.concatenate([parts["mv"].T, parts["mi"].T, parts["mf"].T,
                            jnp.zeros((gate_rows - 2 * ML_HEADS, D), BF16)], axis=0)
    gbias = jnp.broadcast_to(jnp.concatenate([mlstm_gate_bias[0].astype(F32),
                                              jnp.zeros((gate_rows - 2 * ML_HEADS,), F32)])[:, None],
                             (gate_rows, CHUNK))
    gpre = norm_pre[0].astype(F32)[None, :]
    gpost = norm_post[0].astype(F32)[None, :]
    convw = conv_w[0].astype(F32)
    convb = conv_b[0].astype(F32)[None, :]
    hnorm = mlstm_head_norm[0].astype(F32)[None, :]
    wao = w_attn_out[0].astype(BF16)
    wmlo = w_mlstm_out[0].astype(BF16)
    wout = w_out[0].astype(BF16)
    sinks = attn_sinks[0].astype(F32)
    sink_rows = jnp.broadcast_to(jnp.repeat(sinks.reshape(N_HEADS // 2, 2), ATTN_BLOCK, axis=1)[:, None, :],
                                 (N_HEADS // 2, 8, 2 * ATTN_BLOCK))

    xm = jnp.concatenate([jnp.zeros((CHUNK - N_META, D), F32), meta_tokens.astype(F32)], axis=0)
    mpos = jnp.maximum(jnp.arange(CHUNK, dtype=F32) - (CHUNK - N_META), 0.0)
    mrc, mrs1, mrs2 = _rope_tables(mpos)
    kv_w = N_KV * HEAD_DIM
    meta_out = pl.pallas_call(
        _meta_kernel,
        out_shape=(jax.ShapeDtypeStruct((CHUNK, kv_w), F32), jax.ShapeDtypeStruct((CHUNK, kv_w), F32),
                   jax.ShapeDtypeStruct((CONV_HDR, D), F32),
                   jax.ShapeDtypeStruct((ML_HEADS // 2, ML_EXT, 2 * ML_DK), F32),
                   jax.ShapeDtypeStruct((8, LANES), F32)),
        scratch_shapes=[pltpu.VMEM((CONV_HDR + CHUNK, D), F32), pltpu.VMEM((CHUNK, D // 2), BF16),
                        pltpu.VMEM((CHUNK, D // 2), BF16), pltpu.VMEM((ML_HEADS // 2, 1, ML_EXT, 2 * CHUNK), F32),
                        pltpu.VMEM((1, 8, CHUNK), F32), pltpu.VMEM((1, 8, CHUNK), F32),
                        pltpu.VMEM((1, CHUNK, LANES), F32)],
        compiler_params=pltpu.CompilerParams(vmem_limit_bytes=VMEM_LIMIT),
        name="meta_tokens",
    )(xm, gpre, parts["k"], parts["v"], parts["mqk"], wmvt, convw, convb, gbias,
      mrc, mrs1, mrs2)
    kmeta, vmeta, tail0, c0, m0 = meta_out
    km = kmeta[CHUNK - N_META:].reshape(N_META, N_KV, HEAD_DIM).transpose(1, 0, 2)
    km2 = jnp.concatenate([km, km], axis=2).astype(BF16)
    vm = vmeta[CHUNK - N_META:].reshape(N_META, N_KV, HEAD_DIM).transpose(1, 2, 0)
    vmt = jnp.concatenate([vm, jnp.zeros((N_KV, HEAD_DIM, LANES - N_META), F32)], axis=2).astype(BF16)

    rc, rs1, rs2 = _rope_tables(jnp.arange(S, dtype=F32) + N_META)
    rct, rst = rc[:, 0:ROPE_DIM // 2].T, rs2[:, ROPE_DIM // 2:ROPE_DIM].T
    bt, bt0 = _band_bias()
    x_spec = pl.BlockSpec((1, tile, D), lambda b, j: (b, j, 0))
    rope_spec = pl.BlockSpec((tile, LANES), lambda b, j: (j, 0))
    rope_t_spec = pl.BlockSpec((ROPE_DIM // 2, tile), lambda b, j: (0, j))
    seq_params = pltpu.CompilerParams(dimension_semantics=("arbitrary", "arbitrary"),
                                      vmem_limit_bytes=VMEM_LIMIT)
    ya = pl.pallas_call(
        _attn_kernel,
        grid=(B, nt),
        in_specs=[x_spec, _full((1, D)),
                  _full((D, 1024)), _full((D, kv_w)), _full((kv_w, D)), _full((nslice, D, ML_DV)),
                  _full((nslice, D, ML_DV)), _full((D, D)),
                  rope_spec, rope_spec, rope_spec, rope_t_spec, rope_t_spec,
                  _full((N_KV, N_META, LANES)), _full((N_KV, HEAD_DIM, LANES)),
                  _full((BAND, 2 * ATTN_BLOCK)), _full((BAND, 2 * ATTN_BLOCK)),
                  _full((N_HEADS // 2, 8, 2 * ATTN_BLOCK))],
        out_specs=pl.BlockSpec((1, tile, D), lambda b, j: (b, j, 0)),
        out_shape=jax.ShapeDtypeStruct((B, S, D), BF16),
        scratch_shapes=[pltpu.VMEM((tile, D), BF16),
                        pltpu.VMEM((N_HEADS // 2, nblk, 2 * HEAD_DIM, 2 * ATTN_BLOCK), BF16),
                        pltpu.VMEM((N_KV, ATTN_BLOCK + tile, LANES), BF16),
                        pltpu.VMEM((N_KV, nblk + 1, HEAD_DIM, ATTN_BLOCK), BF16),
                        pltpu.VMEM((N_HEADS // 2, tile, LANES), F32),
                        pltpu.VMEM((tile, D), BF16),
                        pltpu.VMEM((nslice, tile, ML_DV), F32),
                        pltpu.VMEM((nslice, tile, ML_DV), F32)],
        compiler_params=seq_params,
        name="attn_branch",
    )(x, gpre, wqt, parts["k"], wvt, sliced(parts["z"]), sliced(parts["ga"]), wao,
      rc, rs1, rs2, rct, rst, km2, vmt, bt, bt0, sink_rows)

    out = pl.pallas_call(
        _mlstm_kernel,
        grid=(B, nt),
        in_specs=[x_spec, pl.BlockSpec((1, tile, D), lambda b, j: (b, j, 0)),
                  _full((1, D)), _full((1, D)),
                  _full((D, 1024)), _full((1024 + gate_rows, D)), _full((nslice, D, ML_DV)),
                  _full((nslice, D, ML_DV)), _full((nslice, D, ML_DV)), _full((D, D)), _full((D, D)),
                  _full((CONV_W, D)), _full((1, D)), _full((gate_rows, CHUNK)), _full((1, D)),
                  _full((CONV_HDR, D)), _full((ML_HEADS // 2, ML_EXT, 2 * ML_DK)), _full((8, LANES))],
        out_specs=pl.BlockSpec((1, tile, D), lambda b, j: (b, j, 0)),
        out_shape=jax.ShapeDtypeStruct((B, S, D), x.dtype),
        scratch_shapes=[pltpu.VMEM((tile, D), BF16),
                        pltpu.VMEM((CONV_HDR + tile, D), F32),
                        pltpu.VMEM((tile, D // 2), BF16),
                        pltpu.VMEM((tile, D // 2), BF16),
                        pltpu.VMEM((ML_HEADS // 2, nchunk, ML_EXT, 2 * CHUNK), F32),
                        pltpu.VMEM((nchunk, 8, CHUNK), F32),
                        pltpu.VMEM((nchunk, 8, CHUNK), F32),
                        pltpu.VMEM((nchunk, CHUNK, LANES), F32),
                        pltpu.VMEM((tile, D), F32),
                        pltpu.VMEM((tile, D), BF16),
                        pltpu.VMEM((tile, D), BF16),
                        pltpu.VMEM((ML_HEADS // 2, ML_EXT, 2 * ML_DK), F32),
                        pltpu.VMEM((8, LANES), F32),
                        pltpu.VMEM((nslice, tile, ML_DV), F32),
                        pltpu.VMEM((nslice, tile, ML_DV), F32),
                        pltpu.VMEM((nslice, tile, ML_DV), F32)],
        compiler_params=seq_params,
        name="mlstm_merge_out",
    )(x, ya, gpre, gpost, parts["mqk"], wmvt, sliced(parts["mo"]), sliced(parts["mz"]),
      sliced(parts["gm"]),
      wmlo, wout, convw, convb, gbias, hnorm, tail0, c0, m0)
    return out
```

```python
import numpy as np
import jax
import jax.numpy as jnp
from jax import lax
from jax.experimental import pallas as pl
from jax.experimental.pallas import tpu as pltpu

F32 = jnp.float32
BF16 = jnp.bfloat16

D_MODEL = 1024
N_META = 16
HEAD_DIM = 64
N_HEADS = 16
N_KV = 4
ATTN_BLOCK = 128
BAND = N_META + 2 * ATTN_BLOCK
ROPE_DIM = 16
ROPE_THETA = 500000.0
ML_HEADS = 4
ML_DV = 256
ML_DK = 128
ML_EXT = ML_DV + 16
CONV_W = 4
CHUNK = 128
RMS_EPS = 1e-6
LN_EPS = 1e-6
NEG_BIG = -1e30
LANES = 128
CONV_HDR = 8
SEQ_TILE = 512
ROW_GROUP = 256
VMEM_LIMIT = 56 * 1024 * 1024


def _dot(a, b):
    return jnp.dot(a, b, preferred_element_type=F32)


def _dot_nt(a, b):
    return lax.dot_general(a, b, (((1,), (1,)), ((), ())), preferred_element_type=F32)


def _sigmoid(x):
    return 0.5 * jnp.tanh(0.5 * x) + 0.5


def _silu(x):
    return x * _sigmoid(x)


def _log_sigmoid(x):
    return jnp.minimum(x, 0.0) - jnp.log(1.0 + jnp.exp(-jnp.abs(x)))


def _rmsnorm(xf, g):
    ms = jnp.mean(xf * xf, axis=-1, keepdims=True)
    return xf * lax.rsqrt(ms + RMS_EPS) * g


def _rope(blk, c_t, s1, s2):
    return blk * c_t + pltpu.roll(blk, LANES - 8, 1) * s1 + pltpu.roll(blk, 8, 1) * s2


def _causal_conv_silu(cs_ref, w_ref, b_ref, row0, rows, cols):
    acc = b_ref[:, cols]
    for tap in range(CONV_W):
        off = CONV_HDR - (CONV_W - 1) + tap + row0
        acc = acc + w_ref[tap:tap + 1, cols] * cs_ref[off:off + rows, cols]
    return _silu(acc)


def _conv_qk(cs_ref, w_ref, b_ref, q_ref, k_ref, row0, rows, valid=None):
    half = D_MODEL // 2
    for cc in range(4):
        cols = slice(256 * cc, 256 * (cc + 1))
        act = _causal_conv_silu(cs_ref, w_ref, b_ref, row0, rows, cols)
        if valid is not None:
            act = jnp.where(valid, act, 0.0)
        if cc < 2:
            q_ref[row0:row0 + rows, cols] = act.astype(BF16)
        else:
            k_ref[row0:row0 + rows, 256 * cc - half:256 * (cc + 1) - half] = (
                act * (ML_DK ** -0.5)).astype(BF16)


def _gate_rows(graw, gbias_ref, valid_t=None):
    g = (graw + gbias_ref[...])[0:8, :]
    sub = lax.broadcasted_iota(jnp.int32, g.shape, 0)
    grow = jnp.where(sub < ML_HEADS, g, _log_sigmoid(g))
    if valid_t is not None:
        grow = jnp.where(valid_t, grow, jnp.where(sub < ML_HEADS, NEG_BIG, 0.0))
    return grow


def _store_gate_forms(c, grow, gr_ref, br_ref, cb_ref):
    lane = lax.broadcasted_iota(jnp.int32, grow.shape, 1)
    brow = grow
    sh = 1
    while sh < CHUNK:
        brow = brow + jnp.where(lane >= sh, pltpu.roll(brow, sh, 1), 0.0)
        sh *= 2
    diff = brow[ML_HEADS:2 * ML_HEADS, :] - grow[0:ML_HEADS, :]
    pad = jnp.zeros((LANES - ML_HEADS, CHUNK), F32)
    gr_ref[c], br_ref[c], cb_ref[c] = grow, brow, jnp.concatenate([diff, pad], axis=0).T


def _mlstm_chunk_all_heads(c, gr_ref, br_ref, cb_ref, q_ref, k_ref, vt_ref, c_ref, m_ref, hs_ref,
                           between=None):
    r0 = c * CHUNK if isinstance(c, int) else pl.multiple_of(c * CHUNK, CHUNK)
    n = CHUNK
    grow, brow, cbm = gr_ref[c], br_ref[c], cb_ref[c]
    row = lax.broadcasted_iota(jnp.int32, (n, 2 * n), 0)
    col = lax.broadcasted_iota(jnp.int32, (n, 2 * n), 1)
    tri2 = row <= (col & (n - 1))
    lo_row = lax.broadcasted_iota(jnp.int32, (1, 2 * n), 1) < n
    lo = lax.broadcasted_iota(jnp.int32, (n, 2 * ML_DK), 1) < ML_DK

    def block_diag(x2):
        zero = jnp.zeros_like(x2)
        return jnp.concatenate([jnp.where(lo, x2, zero), jnp.where(lo, zero, x2)], axis=0)

    staged = []
    for p in range(ML_HEADS // 2):
        a, b = 2 * p, 2 * p + 1
        cols = slice(2 * ML_DK * p, 2 * ML_DK * (p + 1))
        q2, k2, vt2, c2 = q_ref[pl.ds(r0, n), cols], k_ref[pl.ds(r0, n), cols], vt_ref[p, c], c_ref[p]
        li2 = jnp.concatenate([grow[a:a + 1, :], grow[b:b + 1, :]], axis=1)
        b2 = jnp.concatenate([brow[ML_HEADS + a:ML_HEADS + a + 1, :],
                              brow[ML_HEADS + b:ML_HEADS + b + 1, :]], axis=1)
        m_a, m_b = m_ref[a:a + 1, 0:1], m_ref[b:b + 1, 0:1]
        m_prev2 = jnp.where(lo_row, m_a, m_b)
        bl_a, bl_b = b2[:, n - 1:n], b2[:, 2 * n - 1:2 * n]
        b_last2 = jnp.where(lo_row, bl_a, bl_b)
        sc = _dot_nt(jnp.concatenate([k2, c2.astype(BF16)], axis=0), block_diag(q2))
        a_row2 = b_last2 - b2 + li2
        mn_a = jnp.maximum(bl_a + m_a, jnp.max(a_row2[:, :n], axis=1, keepdims=True))
        mn_b = jnp.maximum(bl_b + m_b, jnp.max(a_row2[:, n:], axis=1, keepdims=True))
        m_new2 = jnp.where(lo_row, mn_a, mn_b)
        w_c2 = jnp.exp(b_last2 + m_prev2 - m_new2)
        w_a2 = jnp.exp(a_row2 - m_new2)
        c_ref[p] = w_c2 * c2 + _dot((vt2 * w_a2).astype(BF16), block_diag(k2))
        m_ref[a:a + 1, :] = jnp.broadcast_to(mn_a, (1, LANES))
        m_ref[b:b + 1, :] = jnp.broadcast_to(mn_b, (1, LANES))
        staged.append((sc, vt2, b2, m_prev2))
    if between is not None:
        between()
    if hs_ref is None:
        return
    for p, (sc, vt2, b2, m_prev2) in enumerate(staged):
        a, b = 2 * p, 2 * p + 1
        cb2 = jnp.concatenate([jnp.broadcast_to(cbm[:, a:a + 1], (n, n)),
                               jnp.broadcast_to(cbm[:, b:b + 1], (n, n))], axis=1)
        dt = jnp.where(tri2, b2 - cb2, NEG_BIG)
        inter2 = b2 + m_prev2
        m_row2 = jnp.maximum(inter2, jnp.max(dt, axis=0, keepdims=True))
        sct2 = (sc[0:n] * jnp.exp(dt - m_row2)).astype(BF16)
        het = jnp.exp(inter2 - m_row2) * sc[n:] + _dot(vt2.astype(BF16), block_diag(sct2))
        den = het[ML_DV:ML_DV + 1, :]
        ht2 = het[0:ML_DV, :] / jnp.maximum(jnp.abs(den), jnp.exp(-m_row2))
        hs_ref[pl.ds(r0, n), ML_DV * a:ML_DV * (a + 1)] = ht2[:, :n].T
        hs_ref[pl.ds(r0, n), ML_DV * b:ML_DV * (b + 1)] = ht2[:, n:].T


def _store_values_ext(vt_all, vt_ref):
    nchunk = vt_all.shape[1] // CHUNK
    sub = lax.broadcasted_iota(jnp.int32, (ML_EXT - ML_DV, 2 * CHUNK), 0)
    tail = jnp.where(sub == 0, 1.0, 0.0)
    for p in range(ML_HEADS // 2):
        for c in range(nchunk):
            tok = slice(CHUNK * c, CHUNK * (c + 1))
            for hh in range(2):
                h = 2 * p + hh
                vt_ref[p, c, 0:ML_DV, CHUNK * hh:CHUNK * (hh + 1)] = vt_all[ML_DV * h:ML_DV * (h + 1), tok]
            vt_ref[p, c, ML_DV:ML_EXT, :] = tail


def _meta_kernel(xm_ref, gpre_ref, wk_ref, wv_ref, wmqk_ref, wmvt_ref,
                 convw_ref, convb_ref, gbias_ref, rc_ref, rs1_ref, rs2_ref,
                 kmeta_ref, vmeta_ref, tail_ref, c0_ref, m0_ref,
                 cs_s, q_s, k_s, vt_s, gr_s, br_s, cb_s):
    rows = xm_ref.shape[0]
    u = _rmsnorm(xm_ref[...], gpre_ref[...]).astype(BF16)
    valid = lax.broadcasted_iota(jnp.int32, (rows, 1), 0) >= rows - N_META
    valid_t = lax.broadcasted_iota(jnp.int32, (1, rows), 1) >= rows - N_META

    kc = _dot(u, wk_ref[...])
    for cb in range(kc.shape[1] // LANES):
        sl = slice(LANES * cb, LANES * (cb + 1))
        kmeta_ref[:, sl] = _rope(kc[:, sl], rc_ref[...], rs1_ref[...], rs2_ref[...])
    vmeta_ref[...] = _dot(u, wv_ref[...])

    cs_s[0:CONV_HDR, :] = jnp.zeros((CONV_HDR, D_MODEL), F32)
    cs_s[CONV_HDR:, :] = _dot(u, wmqk_ref[...])
    tail_ref[...] = cs_s[rows:rows + CONV_HDR, :]
    _conv_qk(cs_s, convw_ref, convb_ref, q_s, k_s, 0, rows, valid)

    vt_all = _dot_nt(wmvt_ref[...], u)
    _store_values_ext(jnp.where(valid_t, vt_all, 0.0), vt_s)
    _store_gate_forms(0, _gate_rows(vt_all[ML_HEADS * ML_DV:, :], gbias_ref, valid_t), gr_s, br_s, cb_s)

    c0_ref[...] = jnp.zeros(c0_ref.shape, F32)
    m0_ref[...] = jnp.zeros(m0_ref.shape, F32)
    _mlstm_chunk_all_heads(0, gr_s, br_s, cb_s, q_s, k_s, vt_s, c0_ref, m0_ref, None)


def _attn_kernel(x_ref, gpre_ref, wqt_ref, wk_ref, wvt_ref, wz_ref, wga_ref, wao_ref,
                 rc_ref, rs1_ref, rs2_ref, rct_ref, rst_ref, km2_ref, vmt_ref, bt_ref, bt0_ref, eye_ref, sink_ref,
                 out_ref,
                 u_s, q_s, kk_s, vt_s, a_s, gated_s, zg_s, ga_s):
    j = pl.program_id(1)
    tile = x_ref.shape[1]
    nblk = tile // ATTN_BLOCK
    pairs = N_HEADS // 2

    @pl.when(j == 0)
    def _():
        kk_s[:, 0:ATTN_BLOCK, :] = jnp.zeros((N_KV, ATTN_BLOCK, LANES), BF16)
        vt_s[:, 0] = jnp.zeros((N_KV, HEAD_DIM, ATTN_BLOCK), BF16)

    u_s[...] = _rmsnorm(x_ref[0], gpre_ref[...]).astype(BF16)

    half_r = ROPE_DIM // 2
    cos_t, sin_t = rct_ref[...], rst_ref[...]
    zero_blk = jnp.zeros((HEAD_DIM, ATTN_BLOCK), BF16)
    qt_all = _dot_nt(wqt_ref[...], u_s[...])
    for c in range(pairs):
        qt = qt_all[2 * HEAD_DIM * c:2 * HEAD_DIM * (c + 1), :]
        parts_t = []
        for hh in range(2):
            base = HEAD_DIM * hh
            x1, x2 = qt[base:base + half_r], qt[base + half_r:base + ROPE_DIM]
            parts_t += [x1 * cos_t - x2 * sin_t, x2 * cos_t + x1 * sin_t, qt[base + ROPE_DIM:base + HEAD_DIM]]
        qb = (jnp.concatenate(parts_t, axis=0) * (HEAD_DIM ** -0.5)).astype(BF16)
        for i in range(nblk):
            blk = qb[:, ATTN_BLOCK * i:ATTN_BLOCK * (i + 1)]
            q_s[c, i] = jnp.concatenate(
                [jnp.concatenate([blk[:HEAD_DIM], zero_blk], axis=1),
                 jnp.concatenate([zero_blk, blk[HEAD_DIM:]], axis=1)], axis=0)

    c_t, s1, s2 = rc_ref[...], rs1_ref[...], rs2_ref[...]
    low = lax.broadcasted_iota(jnp.int32, (tile, LANES), 1) < HEAD_DIM

    kc = _dot(u_s[...], wk_ref[...])
    for cb in range(N_KV // 2):
        blk = _rope(kc[:, LANES * cb:LANES * (cb + 1)], c_t, s1, s2)
        swapped = pltpu.roll(blk, HEAD_DIM, 1)
        kk_s[2 * cb, ATTN_BLOCK:, :] = jnp.where(low, blk, swapped).astype(BF16)
        kk_s[2 * cb + 1, ATTN_BLOCK:, :] = jnp.where(low, swapped, blk).astype(BF16)
    vt_all = _dot_nt(wvt_ref[...], u_s[...])
    for g in range(N_KV):
        for bi in range(nblk):
            vt_s[g, 1 + bi] = vt_all[HEAD_DIM * g:HEAD_DIM * (g + 1),
                                     ATTN_BLOCK * bi:ATTN_BLOCK * (bi + 1)].astype(BF16)

    def block_scores(i):
        r0 = i * ATTN_BLOCK
        mask = jnp.where(j == 0, bt0_ref[...], bt_ref[...]) if i == 0 else bt_ref[...]
        out = []
        for c in range(pairs):
            g = c // 2
            keys = jnp.concatenate([kk_s[g, r0:r0 + 2 * ATTN_BLOCK, :], km2_ref[g]], axis=0)
            band = jnp.concatenate([keys, mask], axis=1)
            rhs = jnp.concatenate([q_s[c, i], eye_ref[...]], axis=0)
            out.append(_dot(band, rhs))
        return out

    always = pl.program_id(0) >= 0

    def gate_slice(i, anchors=None):
        k = 0
        for r in range(tile // ROW_GROUP):
            rows = slice(ROW_GROUP * r, ROW_GROUP * (r + 1))
            for w_ref, dst, act in ((wz_ref, zg_s, _silu), (wga_ref, ga_s, _sigmoid)):
                lhs = u_s[rows, :]
                if anchors is not None:
                    head = jnp.where(always, lhs[0:16, 0:LANES], anchors[k])
                    lhs = jnp.concatenate(
                        [jnp.concatenate([head, lhs[0:16, LANES:]], axis=1), lhs[16:]], axis=0)
                dst[i, rows, :] = act(_dot(lhs, w_ref[i]))
                k += 1

    scores = block_scores(0)
    for i in range(nblk):
        r0 = i * ATTN_BLOCK
        next_scores = block_scores(i + 1) if i + 1 < nblk else None
        probs = []
        for c in range(pairs):
            st = scores[c]
            sink = sink_ref[c, 0:1, :]
            mx = jnp.maximum(jnp.max(st, axis=0, keepdims=True), sink)
            p = jnp.exp(st - mx)
            den = jnp.sum(p, axis=0, keepdims=True) + jnp.exp(sink - mx)
            probs.append((p.astype(BF16), 1.0 / den))
        gate_slice(i, [probs[2 * k][0][0:16, 0:LANES] for k in range(4)])
        for c in range(pairs):
            g = c // 2
            pb, rinv = probs[c]
            vt = jnp.concatenate([vt_s[g, i], vt_s[g, i + 1], vmt_ref[g][:, 0:N_META]], axis=1)
            ot = _dot(vt, pb) * rinv
            o = jnp.concatenate([ot[:, :ATTN_BLOCK], ot[:, ATTN_BLOCK:]], axis=0).T
            a_s[c, r0:r0 + ATTN_BLOCK, :] = o
        scores = next_scores

    kk_s[:, 0:ATTN_BLOCK, :] = kk_s[:, tile:tile + ATTN_BLOCK, :]
    vt_s[:, 0] = vt_s[:, nblk]

    width = zg_s.shape[2]
    per = width // LANES
    for g4 in range(nblk):
        att = jnp.concatenate([a_s[per * g4 + p] for p in range(per)], axis=1)
        gated_s[:, width * g4:width * (g4 + 1)] = (att * zg_s[g4]).astype(BF16)
    for g4 in range(nblk):
        cols = slice(width * g4, width * (g4 + 1))
        out_ref[0, :, cols] = (ga_s[g4] * _dot(gated_s[...], wao_ref[:, cols])).astype(out_ref.dtype)


def _mlstm_kernel(x_ref, ya_ref, gpre_ref, gpost_ref, wmqk_ref, wmvt_ref, wmo_ref, wmz_ref,
                  wgm_ref, wmlo_ref, wout_ref, convw_ref, convb_ref, gbias_ref, hnorm_ref,
                  tail0_ref, c0_ref, m0_ref,
                  out_ref,
                  u_s, cs_s, q_s, k_s, vt_s, gr_s, br_s, cb_s, hs_s, hm_s, mg_s, c_s, m_s,
                  og_s, mz_s, gm_s):
    j = pl.program_id(1)
    tile = x_ref.shape[1]

    @pl.when(j == 0)
    def _():
        cs_s[0:CONV_HDR, :] = tail0_ref[...]
        c_s[...] = c0_ref[...]
        m_s[...] = m0_ref[...]

    u_s[...] = _rmsnorm(x_ref[0], gpre_ref[...]).astype(BF16)

    vt_all = _dot_nt(wmvt_ref[...], u_s[...])
    for c in range(tile // CHUNK):
        graw = vt_all[ML_HEADS * ML_DV:, CHUNK * c:CHUNK * (c + 1)]
        _store_gate_forms(c, _gate_rows(graw, gbias_ref), gr_s, br_s, cb_s)
    _store_values_ext(vt_all, vt_s)
    for half in range(2):
        cols = slice(512 * half, 512 * (half + 1))
        cs_s[CONV_HDR:, cols] = _dot(u_s[...], wmqk_ref[:, cols])

    for c in range(tile // CHUNK):
        def gate_slice():
            wcat = jnp.concatenate([wmo_ref[c], wmz_ref[c], wgm_ref[c]], axis=1)
            gates = _dot(u_s[...], wcat)
            og_s[c] = _sigmoid(gates[:, 0:ML_DV])
            mz_s[c] = _silu(gates[:, ML_DV:2 * ML_DV])
            gm_s[c] = _sigmoid(gates[:, 2 * ML_DV:])

        _conv_qk(cs_s, convw_ref, convb_ref, q_s, k_s, CHUNK * c, CHUNK)
        _mlstm_chunk_all_heads(c, gr_s, br_s, cb_s, q_s, k_s, vt_s, c_s, m_s, hs_s, gate_slice)
    cs_s[0:CONV_HDR, :] = cs_s[tile:tile + CONV_HDR, :]

    for r in range(tile // ROW_GROUP):
        rows = slice(ROW_GROUP * r, ROW_GROUP * (r + 1))
        for h in range(ML_HEADS):
            cols = slice(ML_DV * h, ML_DV * (h + 1))
            hg = og_s[h, rows, :] * hs_s[rows, cols]
            hc = hg - jnp.mean(hg, axis=-1, keepdims=True)
            y = hc * lax.rsqrt(jnp.mean(hc * hc, axis=-1, keepdims=True) + LN_EPS) * hnorm_ref[:, cols]
            hm_s[rows, cols] = (y * mz_s[h, rows, :]).astype(BF16)
        for h in range(ML_HEADS):
            cols = slice(ML_DV * h, ML_DV * (h + 1))
            ym = _dot(hm_s[rows, :], wmlo_ref[:, cols])
            mg_s[rows, cols] = (ya_ref[0, rows, cols].astype(F32) + gm_s[h, rows, :] * ym).astype(BF16)
        o = _dot(mg_s[rows, :], wout_ref[...])
        out_ref[0, rows, :] = x_ref[0, rows, :] + _rmsnorm(o, gpost_ref[...])


def _rope_tables(pos):
    half = ROPE_DIM // 2
    inv_freq = ROPE_THETA ** (-jnp.arange(0, ROPE_DIM, 2, dtype=F32) / ROPE_DIM)
    ang = pos[:, None] * inv_freq[None, :]
    cos, sin = jnp.cos(ang), jnp.sin(ang)
    ones = jnp.ones((pos.shape[0], HEAD_DIM - ROPE_DIM), F32)
    zeros_h = jnp.zeros((pos.shape[0], half), F32)
    zeros_r = jnp.zeros_like(ones)
    c_head = jnp.concatenate([cos, cos, ones], axis=1)
    s1_head = jnp.concatenate([-sin, zeros_h, zeros_r], axis=1)
    s2_head = jnp.concatenate([zeros_h, sin, zeros_r], axis=1)
    rep = LANES // HEAD_DIM
    return (jnp.tile(c_head, (1, rep)), jnp.tile(s1_head, (1, rep)), jnp.tile(s2_head, (1, rep)))


def _band_bias():
    kj = np.arange(2 * ATTN_BLOCK)[:, None]
    qi = np.arange(ATTN_BLOCK)[None, :]
    vis = (kj > qi) & (kj <= qi + ATTN_BLOCK)
    meta = np.zeros((N_META, ATTN_BLOCK), np.float32)
    bt = np.concatenate([np.where(vis, 0.0, NEG_BIG), meta], axis=0).astype(np.float32)
    bt0 = np.concatenate([np.where(vis & (kj >= ATTN_BLOCK), 0.0, NEG_BIG), meta], axis=0).astype(np.float32)
    eye2 = np.tile(np.eye(ATTN_BLOCK, dtype=np.float32), (1, 2))
    return jnp.asarray(bt, BF16), jnp.asarray(bt0, BF16), jnp.asarray(eye2, BF16)


def _full(shape, single=True):
    nd = len(shape)
    kw = {"pipeline_mode": pl.Buffered(1)} if single else {}
    return pl.BlockSpec(shape, lambda *_: (0,) * nd, **kw)


def kernel(x, meta_tokens, norm_pre, w_in, attn_sinks, conv_w, conv_b, mlstm_gate_bias, mlstm_head_norm,
           w_attn_out, w_mlstm_out, w_out, norm_post):
    B, S, D = x.shape
    assert D == D_MODEL and w_in.shape[0] == 1 and meta_tokens.shape == (N_META, D_MODEL)
    tile = min(SEQ_TILE, S)
    assert S % tile == 0 and tile % CHUNK == 0 and tile % ATTN_BLOCK == 0
    nt = S // tile
    nblk = tile // ATTN_BLOCK
    nchunk = tile // CHUNK

    w = w_in[0].astype(BF16)
    o = 0
    parts = {}
    for name, width in (("q", 1024), ("k", 256), ("v", 256), ("z", 1024), ("mqk", 1024), ("mv", 1024),
                        ("mi", 4), ("mf", 4), ("mo", 1024), ("mz", 1024), ("ga", 1024), ("gm", 1024)):
        parts[name] = w[:, o:o + width]
        o += width
    nslice = tile // ATTN_BLOCK
    assert nslice == nchunk == ML_HEADS and D // nslice == ML_DV

    def sliced(wm):
        return wm.reshape(D, nslice, wm.shape[1] // nslice).transpose(1, 0, 2)

    wvt = parts["v"].T
    wqt = parts["q"].T
    gate_rows = 16
    wmvt = jnp.concatenate([parts["mv"].T, parts["mi"].T, parts["mf"].T,
                            jnp.zeros((gate_rows - 2 * ML_HEADS, D), BF16)], axis=0)
    gbias = jnp.broadcast_to(jnp.concatenate([mlstm_gate_bias[0].astype(F32),
                                              jnp.zeros((gate_rows - 2 * ML_HEADS,), F32)])[:, None],
                             (gate_rows, CHUNK))
    gpre = norm_pre[0].astype(F32)[None, :]
    gpost = norm_post[0].astype(F32)[None, :]
    convw = conv_w[0].astype(F32)
    convb = conv_b[0].astype(F32)[None, :]
    hnorm = mlstm_head_norm[0].astype(F32)[None, :]
    wao = w_attn_out[0].astype(BF16)
    wmlo = w_mlstm_out[0].astype(BF16)
    wout = w_out[0].astype(BF16)
    sinks = attn_sinks[0].astype(F32)
    sink_rows = jnp.broadcast_to(jnp.repeat(sinks.reshape(N_HEADS // 2, 2), ATTN_BLOCK, axis=1)[:, None, :],
                                 (N_HEADS // 2, 8, 2 * ATTN_BLOCK))

    xm = jnp.concatenate([jnp.zeros((CHUNK - N_META, D), F32), meta_tokens.astype(F32)], axis=0)
    mpos = jnp.maximum(jnp.arange(CHUNK, dtype=F32) - (CHUNK - N_META), 0.0)
    mrc, mrs1, mrs2 = _rope_tables(mpos)
    kv_w = N_KV * HEAD_DIM
    meta_out = pl.pallas_call(
        _meta_kernel,
        out_shape=(jax.ShapeDtypeStruct((CHUNK, kv_w), F32), jax.ShapeDtypeStruct((CHUNK, kv_w), F32),
                   jax.ShapeDtypeStruct((CONV_HDR, D), F32),
                   jax.ShapeDtypeStruct((ML_HEADS // 2, ML_EXT, 2 * ML_DK), F32),
                   jax.ShapeDtypeStruct((8, LANES), F32)),
        scratch_shapes=[pltpu.VMEM((CONV_HDR + CHUNK, D), F32), pltpu.VMEM((CHUNK, D // 2), BF16),
                        pltpu.VMEM((CHUNK, D // 2), BF16), pltpu.VMEM((ML_HEADS // 2, 1, ML_EXT, 2 * CHUNK), F32),
                        pltpu.VMEM((1, 8, CHUNK), F32), pltpu.VMEM((1, 8, CHUNK), F32),
                        pltpu.VMEM((1, CHUNK, LANES), F32)],
        compiler_params=pltpu.CompilerParams(vmem_limit_bytes=VMEM_LIMIT),
        name="meta_tokens",
    )(xm, gpre, parts["k"], parts["v"], parts["mqk"], wmvt, convw, convb, gbias,
      mrc, mrs1, mrs2)
    kmeta, vmeta, tail0, c0, m0 = meta_out
    km = kmeta[CHUNK - N_META:].reshape(N_META, N_KV, HEAD_DIM).transpose(1, 0, 2)
    km2 = jnp.concatenate([km, km], axis=2).astype(BF16)
    vm = vmeta[CHUNK - N_META:].reshape(N_META, N_KV, HEAD_DIM).transpose(1, 2, 0)
    vmt = jnp.concatenate([vm, jnp.zeros((N_KV, HEAD_DIM, LANES - N_META), F32)], axis=2).astype(BF16)

    rc, rs1, rs2 = _rope_tables(jnp.arange(S, dtype=F32) + N_META)
    rct, rst = rc[:, 0:ROPE_DIM // 2].T, rs2[:, ROPE_DIM // 2:ROPE_DIM].T
    bt, bt0, eye2 = _band_bias()
    x_spec = pl.BlockSpec((1, tile, D), lambda b, j: (b, j, 0))
    rope_spec = pl.BlockSpec((tile, LANES), lambda b, j: (j, 0))
    rope_t_spec = pl.BlockSpec((ROPE_DIM // 2, tile), lambda b, j: (0, j))
    seq_params = pltpu.CompilerParams(dimension_semantics=("arbitrary", "arbitrary"),
                                      vmem_limit_bytes=VMEM_LIMIT)
    ya = pl.pallas_call(
        _attn_kernel,
        grid=(B, nt),
        in_specs=[x_spec, _full((1, D)),
                  _full((D, 1024)), _full((D, kv_w)), _full((kv_w, D)), _full((nslice, D, ML_DV)),
                  _full((nslice, D, ML_DV)), _full((D, D)),
                  rope_spec, rope_spec, rope_spec, rope_t_spec, rope_t_spec,
                  _full((N_KV, N_META, LANES)), _full((N_KV, HEAD_DIM, LANES)),
                  _full((BAND, ATTN_BLOCK)), _full((BAND, ATTN_BLOCK)), _full((ATTN_BLOCK, 2 * ATTN_BLOCK)),
                  _full((N_HEADS // 2, 8, 2 * ATTN_BLOCK))],
        out_specs=pl.BlockSpec((1, tile, D), lambda b, j: (b, j, 0)),
        out_shape=jax.ShapeDtypeStruct((B, S, D), BF16),
        scratch_shapes=[pltpu.VMEM((tile, D), BF16),
                        pltpu.VMEM((N_HEADS // 2, nblk, 2 * HEAD_DIM, 2 * ATTN_BLOCK), BF16),
                        pltpu.VMEM((N_KV, ATTN_BLOCK + tile, LANES), BF16),
                        pltpu.VMEM((N_KV, nblk + 1, HEAD_DIM, ATTN_BLOCK), BF16),
                        pltpu.VMEM((N_HEADS // 2, tile, LANES), F32),
                        pltpu.VMEM((tile, D), BF16),
                        pltpu.VMEM((nslice, tile, ML_DV), F32),
                        pltpu.VMEM((nslice, tile, ML_DV), F32)],
        compiler_params=seq_params,
        name="attn_branch",
    )(x, gpre, wqt, parts["k"], wvt, sliced(parts["z"]), sliced(parts["ga"]), wao,
      rc, rs1, rs2, rct, rst, km2, vmt, bt, bt0, eye2, sink_rows)

    out = pl.pallas_call(
        _mlstm_kernel,
        grid=(B, nt),
        in_specs=[x_spec, pl.BlockSpec((1, tile, D), lambda b, j: (b, j, 0)),
                  _full((1, D)), _full((1, D)),
                  _full((D, 1024)), _full((1024 + gate_rows, D)), _full((nslice, D, ML_DV)),
                  _full((nslice, D, ML_DV)), _full((nslice, D, ML_DV)), _full((D, D)), _full((D, D)),
                  _full((CONV_W, D)), _full((1, D)), _full((gate_rows, CHUNK)), _full((1, D)),
                  _full((CONV_HDR, D)), _full((ML_HEADS // 2, ML_EXT, 2 * ML_DK)), _full((8, LANES))],
        out_specs=pl.BlockSpec((1, tile, D), lambda b, j: (b, j, 0)),
        out_shape=jax.ShapeDtypeStruct((B, S, D), x.dtype),
        scratch_shapes=[pltpu.VMEM((tile, D), BF16),
                        pltpu.VMEM((CONV_HDR + tile, D), F32),
                        pltpu.VMEM((tile, D // 2), BF16),
                        pltpu.VMEM((tile, D // 2), BF16),
                        pltpu.VMEM((ML_HEADS // 2, nchunk, ML_EXT, 2 * CHUNK), F32),
                        pltpu.VMEM((nchunk, 8, CHUNK), F32),
                        pltpu.VMEM((nchunk, 8, CHUNK), F32),
                        pltpu.VMEM((nchunk, CHUNK, LANES), F32),
                        pltpu.VMEM((tile, D), F32),
                        pltpu.VMEM((tile, D), BF16),
                        pltpu.VMEM((tile, D), BF16),
                        pltpu.VMEM((ML_HEADS // 2, ML_EXT, 2 * ML_DK), F32),
                        pltpu.VMEM((8, LANES), F32),
                        pltpu.VMEM((nslice, tile, ML_DV), F32),
                        pltpu.VMEM((nslice, tile, ML_DV), F32),
                        pltpu.VMEM((nslice, tile, ML_DV), F32)],
        compiler_params=seq_params,
        name="mlstm_merge_out",
    )(x, ya, gpre, gpost, parts["mqk"], wmvt, sliced(parts["mo"]), sliced(parts["mz"]),
      sliced(parts["gm"]),
      wmlo, wout, convw, convb, gbias, hnorm, tail0, c0, m0)
    return out
```

```python
import numpy as np
import jax
import jax.numpy as jnp
from jax import lax
from jax.experimental import pallas as pl
from jax.experimental.pallas import tpu as pltpu

F32 = jnp.float32
BF16 = jnp.bfloat16

D_MODEL = 1024
N_META = 16
HEAD_DIM = 64
N_HEADS = 16
N_KV = 4
ATTN_BLOCK = 128
BAND = N_META + 2 * ATTN_BLOCK
ROPE_DIM = 16
ROPE_THETA = 500000.0
ML_HEADS = 4
ML_DV = 256
ML_DK = 128
ML_EXT = ML_DV + 16
CONV_W = 4
CHUNK = 128
RMS_EPS = 1e-6
LN_EPS = 1e-6
NEG_BIG = -1e30
LANES = 128
CONV_HDR = 8
SEQ_TILE = 512
ROW_GROUP = 256
VMEM_LIMIT = 56 * 1024 * 1024


def _dot(a, b):
    return jnp.dot(a, b, preferred_element_type=F32)


def _dot_nt(a, b):
    return lax.dot_general(a, b, (((1,), (1,)), ((), ())), preferred_element_type=F32)


def _sigmoid(x):
    return 0.5 * jnp.tanh(0.5 * x) + 0.5


def _silu(x):
    return x * _sigmoid(x)


def _log_sigmoid(x):
    return jnp.minimum(x, 0.0) - jnp.log(1.0 + jnp.exp(-jnp.abs(x)))


def _rmsnorm(xf, g):
    ms = jnp.mean(xf * xf, axis=-1, keepdims=True)
    return xf * lax.rsqrt(ms + RMS_EPS) * g


def _rope(blk, c_t, s1, s2):
    return blk * c_t + pltpu.roll(blk, LANES - 8, 1) * s1 + pltpu.roll(blk, 8, 1) * s2


def _causal_conv_silu(cs_ref, w_ref, b_ref, row0, rows, cols):
    acc = b_ref[:, cols]
    for tap in range(CONV_W):
        off = CONV_HDR - (CONV_W - 1) + tap + row0
        acc = acc + w_ref[tap:tap + 1, cols] * cs_ref[off:off + rows, cols]
    return _silu(acc)


def _conv_qk(cs_ref, w_ref, b_ref, q_ref, k_ref, row0, rows, valid=None):
    half = D_MODEL // 2
    for cc in range(4):
        cols = slice(256 * cc, 256 * (cc + 1))
        act = _causal_conv_silu(cs_ref, w_ref, b_ref, row0, rows, cols)
        if valid is not None:
            act = jnp.where(valid, act, 0.0)
        if cc < 2:
            q_ref[row0:row0 + rows, cols] = act.astype(BF16)
        else:
            k_ref[row0:row0 + rows, 256 * cc - half:256 * (cc + 1) - half] = (
                act * (ML_DK ** -0.5)).astype(BF16)


def _gate_rows(graw, gbias_ref, valid_t=None):
    g = (graw + gbias_ref[...])[0:8, :]
    sub = lax.broadcasted_iota(jnp.int32, g.shape, 0)
    grow = jnp.where(sub < ML_HEADS, g, _log_sigmoid(g))
    if valid_t is not None:
        grow = jnp.where(valid_t, grow, jnp.where(sub < ML_HEADS, NEG_BIG, 0.0))
    return grow


def _store_gate_forms(c, grow, gr_ref, br_ref, cb_ref):
    lane = lax.broadcasted_iota(jnp.int32, grow.shape, 1)
    brow = grow
    sh = 1
    while sh < CHUNK:
        brow = brow + jnp.where(lane >= sh, pltpu.roll(brow, sh, 1), 0.0)
        sh *= 2
    diff = brow[ML_HEADS:2 * ML_HEADS, :] - grow[0:ML_HEADS, :]
    pad = jnp.zeros((LANES - ML_HEADS, CHUNK), F32)
    gr_ref[c], br_ref[c], cb_ref[c] = grow, brow, jnp.concatenate([diff, pad], axis=0).T


def _mlstm_chunk_all_heads(c, gr_ref, br_ref, cb_ref, q_ref, k_ref, vt_ref, c_ref, m_ref, hs_ref,
                           between=None):
    r0 = c * CHUNK if isinstance(c, int) else pl.multiple_of(c * CHUNK, CHUNK)
    n = CHUNK
    grow, brow, cbm = gr_ref[c], br_ref[c], cb_ref[c]
    row = lax.broadcasted_iota(jnp.int32, (n, 2 * n), 0)
    col = lax.broadcasted_iota(jnp.int32, (n, 2 * n), 1)
    tri2 = row <= (col & (n - 1))
    lo_row = lax.broadcasted_iota(jnp.int32, (1, 2 * n), 1) < n
    lo = lax.broadcasted_iota(jnp.int32, (n, 2 * ML_DK), 1) < ML_DK

    def block_diag(x2):
        zero = jnp.zeros_like(x2)
        return jnp.concatenate([jnp.where(lo, x2, zero), jnp.where(lo, zero, x2)], axis=0)

    staged = []
    for p in range(ML_HEADS // 2):
        a, b = 2 * p, 2 * p + 1
        cols = slice(2 * ML_DK * p, 2 * ML_DK * (p + 1))
        q2, k2, vt2, c2 = q_ref[pl.ds(r0, n), cols], k_ref[pl.ds(r0, n), cols], vt_ref[p, c], c_ref[p]
        li2 = jnp.concatenate([grow[a:a + 1, :], grow[b:b + 1, :]], axis=1)
        b2 = jnp.concatenate([brow[ML_HEADS + a:ML_HEADS + a + 1, :],
                              brow[ML_HEADS + b:ML_HEADS + b + 1, :]], axis=1)
        m_a, m_b = m_ref[a:a + 1, 0:1], m_ref[b:b + 1, 0:1]
        m_prev2 = jnp.where(lo_row, m_a, m_b)
        bl_a, bl_b = b2[:, n - 1:n], b2[:, 2 * n - 1:2 * n]
        b_last2 = jnp.where(lo_row, bl_a, bl_b)
        sc = _dot_nt(jnp.concatenate([k2, c2.astype(BF16)], axis=0), block_diag(q2))
        a_row2 = b_last2 - b2 + li2
        mn_a = jnp.maximum(bl_a + m_a, jnp.max(a_row2[:, :n], axis=1, keepdims=True))
        mn_b = jnp.maximum(bl_b + m_b, jnp.max(a_row2[:, n:], axis=1, keepdims=True))
        m_new2 = jnp.where(lo_row, mn_a, mn_b)
        w_c2 = jnp.exp(b_last2 + m_prev2 - m_new2)
        w_a2 = jnp.exp(a_row2 - m_new2)
        c_ref[p] = w_c2 * c2 + _dot((vt2 * w_a2).astype(BF16), block_diag(k2))
        m_ref[a:a + 1, :] = jnp.broadcast_to(mn_a, (1, LANES))
        m_ref[b:b + 1, :] = jnp.broadcast_to(mn_b, (1, LANES))
        staged.append((sc, vt2, b2, m_prev2))
    if between is not None:
        between()
    if hs_ref is None:
        return
    for p, (sc, vt2, b2, m_prev2) in enumerate(staged):
        a, b = 2 * p, 2 * p + 1
        cb2 = jnp.concatenate([jnp.broadcast_to(cbm[:, a:a + 1], (n, n)),
                               jnp.broadcast_to(cbm[:, b:b + 1], (n, n))], axis=1)
        dt = jnp.where(tri2, b2 - cb2, NEG_BIG)
        inter2 = b2 + m_prev2
        m_row2 = jnp.maximum(inter2, jnp.max(dt, axis=0, keepdims=True))
        sct2 = (sc[0:n] * jnp.exp(dt - m_row2)).astype(BF16)
        het = jnp.exp(inter2 - m_row2) * sc[n:] + _dot(vt2.astype(BF16), block_diag(sct2))
        den = het[ML_DV:ML_DV + 1, :]
        ht2 = het[0:ML_DV, :] / jnp.maximum(jnp.abs(den), jnp.exp(-m_row2))
        hs_ref[pl.ds(r0, n), ML_DV * a:ML_DV * (a + 1)] = ht2[:, :n].T
        hs_ref[pl.ds(r0, n), ML_DV * b:ML_DV * (b + 1)] = ht2[:, n:].T


def _store_values_ext(vt_all, vt_ref):
    nchunk = vt_all.shape[1] // CHUNK
    sub = lax.broadcasted_iota(jnp.int32, (ML_EXT - ML_DV, 2 * CHUNK), 0)
    tail = jnp.where(sub == 0, 1.0, 0.0)
    for p in range(ML_HEADS // 2):
        for c in range(nchunk):
            tok = slice(CHUNK * c, CHUNK * (c + 1))
            for hh in range(2):
                h = 2 * p + hh
                vt_ref[p, c, 0:ML_DV, CHUNK * hh:CHUNK * (hh + 1)] = vt_all[ML_DV * h:ML_DV * (h + 1), tok]
            vt_ref[p, c, ML_DV:ML_EXT, :] = tail


def _meta_kernel(xm_ref, gpre_ref, wk_ref, wv_ref, wmqk_ref, wmvt_ref,
                 convw_ref, convb_ref, gbias_ref, rc_ref, rs1_ref, rs2_ref,
                 kmeta_ref, vmeta_ref, tail_ref, c0_ref, m0_ref,
                 cs_s, q_s, k_s, vt_s, gr_s, br_s, cb_s):
    rows = xm_ref.shape[0]
    u = _rmsnorm(xm_ref[...], gpre_ref[...]).astype(BF16)
    valid = lax.broadcasted_iota(jnp.int32, (rows, 1), 0) >= rows - N_META
    valid_t = lax.broadcasted_iota(jnp.int32, (1, rows), 1) >= rows - N_META

    kc = _dot(u, wk_ref[...])
    for cb in range(kc.shape[1] // LANES):
        sl = slice(LANES * cb, LANES * (cb + 1))
        kmeta_ref[:, sl] = _rope(kc[:, sl], rc_ref[...], rs1_ref[...], rs2_ref[...])
    vmeta_ref[...] = _dot(u, wv_ref[...])

    cs_s[0:CONV_HDR, :] = jnp.zeros((CONV_HDR, D_MODEL), F32)
    cs_s[CONV_HDR:, :] = _dot(u, wmqk_ref[...])
    tail_ref[...] = cs_s[rows:rows + CONV_HDR, :]
    _conv_qk(cs_s, convw_ref, convb_ref, q_s, k_s, 0, rows, valid)

    vt_all = _dot_nt(wmvt_ref[...], u)
    _store_values_ext(jnp.where(valid_t, vt_all, 0.0), vt_s)
    _store_gate_forms(0, _gate_rows(vt_all[ML_HEADS * ML_DV:, :], gbias_ref, valid_t), gr_s, br_s, cb_s)

    c0_ref[...] = jnp.zeros(c0_ref.shape, F32)
    m0_ref[...] = jnp.zeros(m0_ref.shape, F32)
    _mlstm_chunk_all_heads(0, gr_s, br_s, cb_s, q_s, k_s, vt_s, c0_ref, m0_ref, None)


def _attn_kernel(x_ref, gpre_ref, wqt_ref, wk_ref, wvt_ref, wz_ref, wga_ref, wao_ref,
                 rc_ref, rs1_ref, rs2_ref, rct_ref, rst_ref, km2_ref, vmt_ref, bt_ref, bt0_ref, eye_ref, sink_ref,
                 out_ref,
                 u_s, q_s, kk_s, vt_s, a_s, gated_s, zg_s, ga_s):
    j = pl.program_id(1)
    tile = x_ref.shape[1]
    nblk = tile // ATTN_BLOCK
    pairs = N_HEADS // 2

    @pl.when(j == 0)
    def _():
        kk_s[:, 0:ATTN_BLOCK, :] = jnp.zeros((N_KV, ATTN_BLOCK, LANES), BF16)
        vt_s[:, 0] = jnp.zeros((N_KV, HEAD_DIM, ATTN_BLOCK), BF16)

    u_s[...] = _rmsnorm(x_ref[0], gpre_ref[...]).astype(BF16)

    half_r = ROPE_DIM // 2
    cos_t, sin_t = rct_ref[...], rst_ref[...]
    zero_blk = jnp.zeros((HEAD_DIM, ATTN_BLOCK), BF16)
    qt_all = _dot_nt(wqt_ref[...], u_s[...])
    for c in range(pairs):
        qt = qt_all[2 * HEAD_DIM * c:2 * HEAD_DIM * (c + 1), :]
        parts_t = []
        for hh in range(2):
            base = HEAD_DIM * hh
            x1, x2 = qt[base:base + half_r], qt[base + half_r:base + ROPE_DIM]
            parts_t += [x1 * cos_t - x2 * sin_t, x2 * cos_t + x1 * sin_t, qt[base + ROPE_DIM:base + HEAD_DIM]]
        qb = (jnp.concatenate(parts_t, axis=0) * (HEAD_DIM ** -0.5)).astype(BF16)
        for i in range(nblk):
            blk = qb[:, ATTN_BLOCK * i:ATTN_BLOCK * (i + 1)]
            q_s[c, i] = jnp.concatenate(
                [jnp.concatenate([blk[:HEAD_DIM], zero_blk], axis=1),
                 jnp.concatenate([zero_blk, blk[HEAD_DIM:]], axis=1)], axis=0)

    c_t, s1, s2 = rc_ref[...], rs1_ref[...], rs2_ref[...]
    low = lax.broadcasted_iota(jnp.int32, (tile, LANES), 1) < HEAD_DIM

    kc = _dot(u_s[...], wk_ref[...])
    for cb in range(N_KV // 2):
        blk = _rope(kc[:, LANES * cb:LANES * (cb + 1)], c_t, s1, s2)
        swapped = pltpu.roll(blk, HEAD_DIM, 1)
        kk_s[2 * cb, ATTN_BLOCK:, :] = jnp.where(low, blk, swapped).astype(BF16)
        kk_s[2 * cb + 1, ATTN_BLOCK:, :] = jnp.where(low, swapped, blk).astype(BF16)
    vt_all = _dot_nt(wvt_ref[...], u_s[...])
    for g in range(N_KV):
        for bi in range(nblk):
            vt_s[g, 1 + bi] = vt_all[HEAD_DIM * g:HEAD_DIM * (g + 1),
                                     ATTN_BLOCK * bi:ATTN_BLOCK * (bi + 1)].astype(BF16)

    def block_scores(i):
        r0 = i * ATTN_BLOCK
        mask = jnp.where(j == 0, bt0_ref[...], bt_ref[...]) if i == 0 else bt_ref[...]
        out = []
        for c in range(pairs):
            g = c // 2
            keys = jnp.concatenate([kk_s[g, r0:r0 + 2 * ATTN_BLOCK, :], km2_ref[g]], axis=0)
            band = jnp.concatenate([keys, mask], axis=1)
            rhs = jnp.concatenate([q_s[c, i], eye_ref[...]], axis=0)
            out.append(_dot(band, rhs))
        return out

    ones_rows = jnp.where(lax.broadcasted_iota(jnp.int32, (16, BAND), 0) == 0, 1.0, 0.0).astype(BF16)
    always = pl.program_id(0) >= 0

    def gate_slice(i, anchors=None):
        k = 0
        for r in range(tile // ROW_GROUP):
            rows = slice(ROW_GROUP * r, ROW_GROUP * (r + 1))
            for w_ref, dst, act in ((wz_ref, zg_s, _silu), (wga_ref, ga_s, _sigmoid)):
                lhs = u_s[rows, :]
                if anchors is not None:
                    head = jnp.where(always, lhs[0:16, 0:LANES], anchors[k])
                    lhs = jnp.concatenate(
                        [jnp.concatenate([head, lhs[0:16, LANES:]], axis=1), lhs[16:]], axis=0)
                dst[i, rows, :] = act(_dot(lhs, w_ref[i]))
                k += 1

    scores = block_scores(0)
    for i in range(nblk):
        r0 = i * ATTN_BLOCK
        next_scores = block_scores(i + 1) if i + 1 < nblk else None
        probs = []
        for c in range(pairs):
            pbs, sink_terms = [], []
            for hh in range(2):
                lanes = slice(ATTN_BLOCK * hh, ATTN_BLOCK * (hh + 1))
                st = scores[c][:, lanes]
                sink = sink_ref[c, 0:1, lanes]
                mx = jnp.maximum(jnp.max(st, axis=0, keepdims=True), sink)
                pbs.append(jnp.exp(st - mx).astype(BF16))
                sink_terms.append(jnp.exp(sink - mx))
            probs.append((jnp.concatenate(pbs, axis=1), jnp.concatenate(sink_terms, axis=1)))
        gate_slice(i, [probs[2 * k][0][0:16, 0:LANES] for k in range(4)])
        for c in range(pairs):
            g = c // 2
            pb, sink_term = probs[c]
            vt = jnp.concatenate([vt_s[g, i], vt_s[g, i + 1], vmt_ref[g][:, 0:N_META]], axis=1)
            ot = _dot(jnp.concatenate([vt, ones_rows], axis=0), pb)
            ot = ot[0:HEAD_DIM] * (1.0 / (ot[HEAD_DIM:HEAD_DIM + 1] + sink_term))
            o = jnp.concatenate([ot[:, :ATTN_BLOCK], ot[:, ATTN_BLOCK:]], axis=0).T
            a_s[c, r0:r0 + ATTN_BLOCK, :] = o
        scores = next_scores

    kk_s[:, 0:ATTN_BLOCK, :] = kk_s[:, tile:tile + ATTN_BLOCK, :]
    vt_s[:, 0] = vt_s[:, nblk]

    width = zg_s.shape[2]
    per = width // LANES
    for g4 in range(nblk):
        att = jnp.concatenate([a_s[per * g4 + p] for p in range(per)], axis=1)
        gated_s[:, width * g4:width * (g4 + 1)] = (att * zg_s[g4]).astype(BF16)
    for g4 in range(nblk):
        cols = slice(width * g4, width * (g4 + 1))
        out_ref[0, :, cols] = (ga_s[g4] * _dot(gated_s[...], wao_ref[:, cols])).astype(out_ref.dtype)


def _mlstm_kernel(x_ref, ya_ref, gpre_ref, gpost_ref, wmqk_ref, wmvt_ref, wmo_ref, wmz_ref,
                  wgm_ref, wmlo_ref, wout_ref, convw_ref, convb_ref, gbias_ref, hnorm_ref,
                  tail0_ref, c0_ref, m0_ref,
                  out_ref,
                  u_s, cs_s, q_s, k_s, vt_s, gr_s, br_s, cb_s, hs_s, hm_s, mg_s, c_s, m_s,
                  og_s, mz_s, gm_s):
    j = pl.program_id(1)
    tile = x_ref.shape[1]

    @pl.when(j == 0)
    def _():
        cs_s[0:CONV_HDR, :] = tail0_ref[...]
        c_s[...] = c0_ref[...]
        m_s[...] = m0_ref[...]

    u_s[...] = _rmsnorm(x_ref[0], gpre_ref[...]).astype(BF16)

    vt_all = _dot_nt(wmvt_ref[...], u_s[...])
    for c in range(tile // CHUNK):
        graw = vt_all[ML_HEADS * ML_DV:, CHUNK * c:CHUNK * (c + 1)]
        _store_gate_forms(c, _gate_rows(graw, gbias_ref), gr_s, br_s, cb_s)
    _store_values_ext(vt_all, vt_s)
    for half in range(2):
        cols = slice(512 * half, 512 * (half + 1))
        cs_s[CONV_HDR:, cols] = _dot(u_s[...], wmqk_ref[:, cols])

    for c in range(tile // CHUNK):
        def gate_slice():
            wcat = jnp.concatenate([wmo_ref[c], wmz_ref[c], wgm_ref[c]], axis=1)
            gates = _dot(u_s[...], wcat)
            og_s[c] = _sigmoid(gates[:, 0:ML_DV])
            mz_s[c] = _silu(gates[:, ML_DV:2 * ML_DV])
            gm_s[c] = _sigmoid(gates[:, 2 * ML_DV:])

        _conv_qk(cs_s, convw_ref, convb_ref, q_s, k_s, CHUNK * c, CHUNK)
        _mlstm_chunk_all_heads(c, gr_s, br_s, cb_s, q_s, k_s, vt_s, c_s, m_s, hs_s, gate_slice)
    cs_s[0:CONV_HDR, :] = cs_s[tile:tile + CONV_HDR, :]

    for r in range(tile // ROW_GROUP):
        rows = slice(ROW_GROUP * r, ROW_GROUP * (r + 1))
        for h in range(ML_HEADS):
            cols = slice(ML_DV * h, ML_DV * (h + 1))
            hg = og_s[h, rows, :] * hs_s[rows, cols]
            hc = hg - jnp.mean(hg, axis=-1, keepdims=True)
            y = hc * lax.rsqrt(jnp.mean(hc * hc, axis=-1, keepdims=True) + LN_EPS) * hnorm_ref[:, cols]
            hm_s[rows, cols] = (y * mz_s[h, rows, :]).astype(BF16)
        for h in range(ML_HEADS):
            cols = slice(ML_DV * h, ML_DV * (h + 1))
            ym = _dot(hm_s[rows, :], wmlo_ref[:, cols])
            mg_s[rows, cols] = (ya_ref[0, rows, cols].astype(F32) + gm_s[h, rows, :] * ym).astype(BF16)
        o = _dot(mg_s[rows, :], wout_ref[...])
        out_ref[0, rows, :] = x_ref[0, rows, :] + _rmsnorm(o, gpost_ref[...])


def _rope_tables(pos):
    half = ROPE_DIM // 2
    inv_freq = ROPE_THETA ** (-jnp.arange(0, ROPE_DIM, 2, dtype=F32) / ROPE_DIM)
    ang = pos[:, None] * inv_freq[None, :]
    cos, sin = jnp.cos(ang), jnp.sin(ang)
    ones = jnp.ones((pos.shape[0], HEAD_DIM - ROPE_DIM), F32)
    zeros_h = jnp.zeros((pos.shape[0], half), F32)
    zeros_r = jnp.zeros_like(ones)
    c_head = jnp.concatenate([cos, cos, ones], axis=1)
    s1_head = jnp.concatenate([-sin, zeros_h, zeros_r], axis=1)
    s2_head = jnp.concatenate([zeros_h, sin, zeros_r], axis=1)
    rep = LANES // HEAD_DIM
    return (jnp.tile(c_head, (1, rep)), jnp.tile(s1_head, (1, rep)), jnp.tile(s2_head, (1, rep)))


def _band_bias():
    kj = np.arange(2 * ATTN_BLOCK)[:, None]
    qi = np.arange(ATTN_BLOCK)[None, :]
    vis = (kj > qi) & (kj <= qi + ATTN_BLOCK)
    meta = np.zeros((N_META, ATTN_BLOCK), np.float32)
    bt = np.concatenate([np.where(vis, 0.0, NEG_BIG), meta], axis=0).astype(np.float32)
    bt0 = np.concatenate([np.where(vis & (kj >= ATTN_BLOCK), 0.0, NEG_BIG), meta], axis=0).astype(np.float32)
    eye2 = np.tile(np.eye(ATTN_BLOCK, dtype=np.float32), (1, 2))
    return jnp.asarray(bt, BF16), jnp.asarray(bt0, BF16), jnp.asarray(eye2, BF16)


def _full(shape, single=True):
    nd = len(shape)
    kw = {"pipeline_mode": pl.Buffered(1)} if single else {}
    return pl.BlockSpec(shape, lambda *_: (0,) * nd, **kw)


def kernel(x, meta_tokens, norm_pre, w_in, attn_sinks, conv_w, conv_b, mlstm_gate_bias, mlstm_head_norm,
           w_attn_out, w_mlstm_out, w_out, norm_post):
    B, S, D = x.shape
    assert D == D_MODEL and w_in.shape[0] == 1 and meta_tokens.shape == (N_META, D_MODEL)
    tile = min(SEQ_TILE, S)
    assert S % tile == 0 and tile % CHUNK == 0 and tile % ATTN_BLOCK == 0
    nt = S // tile
    nblk = tile // ATTN_BLOCK
    nchunk = tile // CHUNK

    w = w_in[0].astype(BF16)
    o = 0
    parts = {}
    for name, width in (("q", 1024), ("k", 256), ("v", 256), ("z", 1024), ("mqk", 1024), ("mv", 1024),
                        ("mi", 4), ("mf", 4), ("mo", 1024), ("mz", 1024), ("ga", 1024), ("gm", 1024)):
        parts[name] = w[:, o:o + width]
        o += width
    nslice = tile // ATTN_BLOCK
    assert nslice == nchunk == ML_HEADS and D // nslice == ML_DV

    def sliced(wm):
        return wm.reshape(D, nslice, wm.shape[1] // nslice).transpose(1, 0, 2)

    wvt = parts["v"].T
    wqt = parts["q"].T
    gate_rows = 16
    wmvt = jnp.concatenate([parts["mv"].T, parts["mi"].T, parts["mf"].T,
                            jnp.zeros((gate_rows - 2 * ML_HEADS, D), BF16)], axis=0)
    gbias = jnp.broadcast_to(jnp.concatenate([mlstm_gate_bias[0].astype(F32),
                                              jnp.zeros((gate_rows - 2 * ML_HEADS,), F32)])[:, None],
                             (gate_rows, CHUNK))
    gpre = norm_pre[0].astype(F32)[None, :]
    gpost = norm_post[0].astype(F32)[None, :]
    convw = conv_w[0].astype(F32)
    convb = conv_b[0].astype(F32)[None, :]
    hnorm = mlstm_head_norm[0].astype(F32)[None, :]
    wao = w_attn_out[0].astype(BF16)
    wmlo = w_mlstm_out[0].astype(BF16)
    wout = w_out[0].astype(BF16)
    sinks = attn_sinks[0].astype(F32)
    sink_rows = jnp.broadcast_to(jnp.repeat(sinks.reshape(N_HEADS // 2, 2), ATTN_BLOCK, axis=1)[:, None, :],
                                 (N_HEADS // 2, 8, 2 * ATTN_BLOCK))

    xm = jnp.concatenate([jnp.zeros((CHUNK - N_META, D), F32), meta_tokens.astype(F32)], axis=0)
    mpos = jnp.maximum(jnp.arange(CHUNK, dtype=F32) - (CHUNK - N_META), 0.0)
    mrc, mrs1, mrs2 = _rope_tables(mpos)
    kv_w = N_KV * HEAD_DIM
    meta_out = pl.pallas_call(
        _meta_kernel,
        out_shape=(jax.ShapeDtypeStruct((CHUNK, kv_w), F32), jax.ShapeDtypeStruct((CHUNK, kv_w), F32),
                   jax.ShapeDtypeStruct((CONV_HDR, D), F32),
                   jax.ShapeDtypeStruct((ML_HEADS // 2, ML_EXT, 2 * ML_DK), F32),
                   jax.ShapeDtypeStruct((8, LANES), F32)),
        scratch_shapes=[pltpu.VMEM((CONV_HDR + CHUNK, D), F32), pltpu.VMEM((CHUNK, D // 2), BF16),
                        pltpu.VMEM((CHUNK, D // 2), BF16), pltpu.VMEM((ML_HEADS // 2, 1, ML_EXT, 2 * CHUNK), F32),
                        pltpu.VMEM((1, 8, CHUNK), F32), pltpu.VMEM((1, 8, CHUNK), F32),
                        pltpu.VMEM((1, CHUNK, LANES), F32)],
        compiler_params=pltpu.CompilerParams(vmem_limit_bytes=VMEM_LIMIT),
        name="meta_tokens",
    )(xm, gpre, parts["k"], parts["v"], parts["mqk"], wmvt, convw, convb, gbias,
      mrc, mrs1, mrs2)
    kmeta, vmeta, tail0, c0, m0 = meta_out
    km = kmeta[CHUNK - N_META:].reshape(N_META, N_KV, HEAD_DIM).transpose(1, 0, 2)
    km2 = jnp.concatenate([km, km], axis=2).astype(BF16)
    vm = vmeta[CHUNK - N_META:].reshape(N_META, N_KV, HEAD_DIM).transpose(1, 2, 0)
    vmt = jnp.concatenate([vm, jnp.zeros((N_KV, HEAD_DIM, LANES - N_META), F32)], axis=2).astype(BF16)

    rc, rs1, rs2 = _rope_tables(jnp.arange(S, dtype=F32) + N_META)
    rct, rst = rc[:, 0:ROPE_DIM // 2].T, rs2[:, ROPE_DIM // 2:ROPE_DIM].T
    bt, bt0, eye2 = _band_bias()
    x_spec = pl.BlockSpec((1, tile, D), lambda b, j: (b, j, 0))
    rope_spec = pl.BlockSpec((tile, LANES), lambda b, j: (j, 0))
    rope_t_spec = pl.BlockSpec((ROPE_DIM // 2, tile), lambda b, j: (0, j))
    seq_params = pltpu.CompilerParams(dimension_semantics=("arbitrary", "arbitrary"),
                                      vmem_limit_bytes=VMEM_LIMIT)
    ya = pl.pallas_call(
        _attn_kernel,
        grid=(B, nt),
        in_specs=[x_spec, _full((1, D)),
                  _full((D, 1024)), _full((D, kv_w)), _full((kv_w, D)), _full((nslice, D, ML_DV)),
                  _full((nslice, D, ML_DV)), _full((D, D)),
                  rope_spec, rope_spec, rope_spec, rope_t_spec, rope_t_spec,
                  _full((N_KV, N_META, LANES)), _full((N_KV, HEAD_DIM, LANES)),
                  _full((BAND, ATTN_BLOCK)), _full((BAND, ATTN_BLOCK)), _full((ATTN_BLOCK, 2 * ATTN_BLOCK)),
                  _full((N_HEADS // 2, 8, 2 * ATTN_BLOCK))],
        out_specs=pl.BlockSpec((1, tile, D), lambda b, j: (b, j, 0)),
        out_shape=jax.ShapeDtypeStruct((B, S, D), BF16),
        scratch_shapes=[pltpu.VMEM((tile, D), BF16),
                        pltpu.VMEM((N_HEADS // 2, nblk, 2 * HEAD_DIM, 2 * ATTN_BLOCK), BF16),
                        pltpu.VMEM((N_KV, ATTN_BLOCK + tile, LANES), BF16),
                        pltpu.VMEM((N_KV, nblk + 1, HEAD_DIM, ATTN_BLOCK), BF16),
                        pltpu.VMEM((N_HEADS // 2, tile, LANES), F32),
                        pltpu.VMEM((tile, D), BF16),
                        pltpu.VMEM((nslice, tile, ML_DV), F32),
                        pltpu.VMEM((nslice, tile, ML_DV), F32)],
        compiler_params=seq_params,
        name="attn_branch",
    )(x, gpre, wqt, parts["k"], wvt, sliced(parts["z"]), sliced(parts["ga"]), wao,
      rc, rs1, rs2, rct, rst, km2, vmt, bt, bt0, eye2, sink_rows)

    out = pl.pallas_call(
        _mlstm_kernel,
        grid=(B, nt),
        in_specs=[x_spec, pl.BlockSpec((1, tile, D), lambda b, j: (b, j, 0)),
                  _full((1, D)), _full((1, D)),
                  _full((D, 1024)), _full((1024 + gate_rows, D)), _full((nslice, D, ML_DV)),
                  _full((nslice, D, ML_DV)), _full((nslice, D, ML_DV)), _full((D, D)), _full((D, D)),
                  _full((CONV_W, D)), _full((1, D)), _full((gate_rows, CHUNK)), _full((1, D)),
                  _full((CONV_HDR, D)), _full((ML_HEADS // 2, ML_EXT, 2 * ML_DK)), _full((8, LANES))],
        out_specs=pl.BlockSpec((1, tile, D), lambda b, j: (b, j, 0)),
        out_shape=jax.ShapeDtypeStruct((B, S, D), x.dtype),
        scratch_shapes=[pltpu.VMEM((tile, D), BF16),
                        pltpu.VMEM((CONV_HDR + tile, D), F32),
                        pltpu.VMEM((tile, D // 2), BF16),
                        pltpu.VMEM((tile, D // 2), BF16),
                        pltpu.VMEM((ML_HEADS // 2, nchunk, ML_EXT, 2 * CHUNK), F32),
                        pltpu.VMEM((nchunk, 8, CHUNK), F32),
                        pltpu.VMEM((nchunk, 8, CHUNK), F32),
                        pltpu.VMEM((nchunk, CHUNK, LANES), F32),
                        pltpu.VMEM((tile, D), F32),
                        pltpu.VMEM((tile, D), BF16),
                        pltpu.VMEM((tile, D), BF16),
                        pltpu.VMEM((ML_HEADS // 2, ML_EXT, 2 * ML_DK), F32),
                        pltpu.VMEM((8, LANES), F32),
                        pltpu.VMEM((nslice, tile, ML_DV), F32),
                        pltpu.VMEM((nslice, tile, ML_DV), F32),
                        pltpu.VMEM((nslice, tile, ML_DV), F32)],
        compiler_params=seq_params,
        name="mlstm_merge_out",
    )(x, ya, gpre, gpost, parts["mqk"], wmvt, sliced(parts["mo"]), sliced(parts["mz"]),
      sliced(parts["gm"]),
      wmlo, wout, convw, convb, gbias, hnorm, tail0, c0, m0)
    return out
```

```python
import numpy as np
import jax
import jax.numpy as jnp
from jax import lax
from jax.experimental import pallas as pl
from jax.experimental.pallas import tpu as pltpu

F32 = jnp.float32
BF16 = jnp.bfloat16

D_MODEL = 1024
N_META = 16
HEAD_DIM = 64
N_HEADS = 16
N_KV = 4
ATTN_BLOCK = 128
BAND = N_META + 2 * ATTN_BLOCK
ROPE_DIM = 16
ROPE_THETA = 500000.0
ML_HEADS = 4
ML_DV = 256
ML_DK = 128
BF16_ROWS = 16
ML_EXT = ML_DV + BF16_ROWS
CONV_W = 4
CHUNK = 128
RMS_EPS = 1e-6
LN_EPS = 1e-6
NEG_BIG = -1e30
LANES = 128
CONV_HDR = 8
CONV_COLS = 256
SEQ_TILE = 512
ROW_GROUP = 256
VMEM_LIMIT = 56 * 1024 * 1024


def _dot(a, b):
    return jnp.dot(a, b, preferred_element_type=F32)


def _dot_nt(a, b):
    return lax.dot_general(a, b, (((1,), (1,)), ((), ())), preferred_element_type=F32)


def _sigmoid(x):
    return 0.5 * jnp.tanh(0.5 * x) + 0.5


def _silu(x):
    return x * _sigmoid(x)


def _log_sigmoid(x):
    return jnp.minimum(x, 0.0) - jnp.log(1.0 + jnp.exp(-jnp.abs(x)))


def _rmsnorm(xf, g):
    ms = jnp.mean(xf * xf, axis=-1, keepdims=True)
    return xf * lax.rsqrt(ms + RMS_EPS) * g


def _rope(blk, c_t, s1, s2):
    return blk * c_t + pltpu.roll(blk, LANES - 8, 1) * s1 + pltpu.roll(blk, 8, 1) * s2


def _causal_conv_silu(cs_ref, w_ref, b_ref, row0, rows, cols):
    acc = b_ref[:, cols]
    for tap in range(CONV_W):
        off = CONV_HDR - (CONV_W - 1) + tap + row0
        acc = acc + w_ref[tap:tap + 1, cols] * cs_ref[off:off + rows, cols]
    return _silu(acc)


def _conv_qk(cs_ref, w_ref, b_ref, q_ref, k_ref, row0, rows, valid=None):
    half = D_MODEL // 2
    for cc in range(D_MODEL // CONV_COLS):
        cols = slice(CONV_COLS * cc, CONV_COLS * (cc + 1))
        act = _causal_conv_silu(cs_ref, w_ref, b_ref, row0, rows, cols)
        if valid is not None:
            act = jnp.where(valid, act, 0.0)
        if CONV_COLS * cc < half:
            q_ref[row0:row0 + rows, cols] = act.astype(BF16)
        else:
            k_ref[row0:row0 + rows, CONV_COLS * cc - half:CONV_COLS * (cc + 1) - half] = (
                act * (ML_DK ** -0.5)).astype(BF16)


def _gate_rows(graw, gbias_ref, valid_t=None):
    g = (graw + gbias_ref[...])[0:8, :]
    sub = lax.broadcasted_iota(jnp.int32, g.shape, 0)
    grow = jnp.where(sub < ML_HEADS, g, _log_sigmoid(g))
    if valid_t is not None:
        grow = jnp.where(valid_t, grow, jnp.where(sub < ML_HEADS, NEG_BIG, 0.0))
    return grow


def _store_gate_forms(c, grow, gr_ref, br_ref, cb_ref):
    lane = lax.broadcasted_iota(jnp.int32, grow.shape, 1)
    brow = grow
    sh = 1
    while sh < CHUNK:
        brow = brow + jnp.where(lane >= sh, pltpu.roll(brow, sh, 1), 0.0)
        sh *= 2
    diff = brow[ML_HEADS:2 * ML_HEADS, :] - grow[0:ML_HEADS, :]
    pad = jnp.zeros((LANES - ML_HEADS, CHUNK), F32)
    gr_ref[c], br_ref[c], cb_ref[c] = grow, brow, jnp.concatenate([diff, pad], axis=0).T


def _mlstm_chunk_all_heads(c, gr_ref, br_ref, cb_ref, q_ref, k_ref, vt_ref, c_ref, m_ref, hs_ref,
                           between=None):
    r0 = c * CHUNK if isinstance(c, int) else pl.multiple_of(c * CHUNK, CHUNK)
    n = CHUNK
    grow, brow, cbm = gr_ref[c], br_ref[c], cb_ref[c]
    row = lax.broadcasted_iota(jnp.int32, (n, 2 * n), 0)
    col = lax.broadcasted_iota(jnp.int32, (n, 2 * n), 1)
    tri2 = row <= (col & (n - 1))
    lo_row = lax.broadcasted_iota(jnp.int32, (1, 2 * n), 1) < n
    lo = lax.broadcasted_iota(jnp.int32, (n, 2 * ML_DK), 1) < ML_DK

    def block_diag(x2):
        zero = jnp.zeros_like(x2)
        return jnp.concatenate([jnp.where(lo, x2, zero), jnp.where(lo, zero, x2)], axis=0)

    staged = []
    for p in range(ML_HEADS // 2):
        a, b = 2 * p, 2 * p + 1
        cols = slice(2 * ML_DK * p, 2 * ML_DK * (p + 1))
        q2, k2, vt2, c2 = q_ref[pl.ds(r0, n), cols], k_ref[pl.ds(r0, n), cols], vt_ref[p, c], c_ref[p]
        li2 = jnp.concatenate([grow[a:a + 1, :], grow[b:b + 1, :]], axis=1)
        b2 = jnp.concatenate([brow[ML_HEADS + a:ML_HEADS + a + 1, :],
                              brow[ML_HEADS + b:ML_HEADS + b + 1, :]], axis=1)
        m_a, m_b = m_ref[a:a + 1, 0:1], m_ref[b:b + 1, 0:1]
        m_prev2 = jnp.where(lo_row, m_a, m_b)
        bl_a, bl_b = b2[:, n - 1:n], b2[:, 2 * n - 1:2 * n]
        b_last2 = jnp.where(lo_row, bl_a, bl_b)
        sc = _dot_nt(jnp.concatenate([k2, c2.astype(BF16)], axis=0), block_diag(q2))
        a_row2 = b_last2 - b2 + li2
        mn_a = jnp.maximum(bl_a + m_a, jnp.max(a_row2[:, :n], axis=1, keepdims=True))
        mn_b = jnp.maximum(bl_b + m_b, jnp.max(a_row2[:, n:], axis=1, keepdims=True))
        m_new2 = jnp.where(lo_row, mn_a, mn_b)
        w_c2 = jnp.exp(b_last2 + m_prev2 - m_new2)
        w_a2 = jnp.exp(a_row2 - m_new2)
        c_ref[p] = w_c2 * c2 + _dot((vt2 * w_a2).astype(BF16), block_diag(k2))
        m_ref[a:a + 1, :] = jnp.broadcast_to(mn_a, (1, LANES))
        m_ref[b:b + 1, :] = jnp.broadcast_to(mn_b, (1, LANES))
        staged.append((sc, vt2, b2, m_prev2))
    if between is not None:
        between()
    if hs_ref is None:
        return
    for p, (sc, vt2, b2, m_prev2) in enumerate(staged):
        a, b = 2 * p, 2 * p + 1
        cb2 = jnp.concatenate([jnp.broadcast_to(cbm[:, a:a + 1], (n, n)),
                               jnp.broadcast_to(cbm[:, b:b + 1], (n, n))], axis=1)
        dt = jnp.where(tri2, b2 - cb2, NEG_BIG)
        inter2 = b2 + m_prev2
        m_row2 = jnp.maximum(inter2, jnp.max(dt, axis=0, keepdims=True))
        sct2 = (sc[0:n] * jnp.exp(dt - m_row2)).astype(BF16)
        het = jnp.exp(inter2 - m_row2) * sc[n:] + _dot(vt2.astype(BF16), block_diag(sct2))
        den = het[ML_DV:ML_DV + 1, :]
        ht2 = het[0:ML_DV, :] / jnp.maximum(jnp.abs(den), jnp.exp(-m_row2))
        hs_ref[pl.ds(r0, n), ML_DV * a:ML_DV * (a + 1)] = ht2[:, :n].T
        hs_ref[pl.ds(r0, n), ML_DV * b:ML_DV * (b + 1)] = ht2[:, n:].T


def _store_values_ext(vt_all, vt_ref):
    nchunk = vt_all.shape[1] // CHUNK
    sub = lax.broadcasted_iota(jnp.int32, (ML_EXT - ML_DV, 2 * CHUNK), 0)
    tail = jnp.where(sub == 0, 1.0, 0.0)
    for p in range(ML_HEADS // 2):
        for c in range(nchunk):
            tok = slice(CHUNK * c, CHUNK * (c + 1))
            for hh in range(2):
                h = 2 * p + hh
                vt_ref[p, c, 0:ML_DV, CHUNK * hh:CHUNK * (hh + 1)] = vt_all[ML_DV * h:ML_DV * (h + 1), tok]
            vt_ref[p, c, ML_DV:ML_EXT, :] = tail


def _meta_kernel(xm_ref, gpre_ref, wk_ref, wv_ref, wmqk_ref, wmvt_ref,
                 convw_ref, convb_ref, gbias_ref, rc_ref, rs1_ref, rs2_ref,
                 kmeta_ref, vmeta_ref, tail_ref, c0_ref, m0_ref,
                 cs_s, q_s, k_s, vt_s, gr_s, br_s, cb_s):
    rows = xm_ref.shape[0]
    u = _rmsnorm(xm_ref[...], gpre_ref[...]).astype(BF16)
    valid = lax.broadcasted_iota(jnp.int32, (rows, 1), 0) >= rows - N_META
    valid_t = lax.broadcasted_iota(jnp.int32, (1, rows), 1) >= rows - N_META

    kc = _dot(u, wk_ref[...])
    for cb in range(kc.shape[1] // LANES):
        sl = slice(LANES * cb, LANES * (cb + 1))
        kmeta_ref[:, sl] = _rope(kc[:, sl], rc_ref[...], rs1_ref[...], rs2_ref[...])
    vmeta_ref[...] = _dot(u, wv_ref[...])

    cs_s[0:CONV_HDR, :] = jnp.zeros((CONV_HDR, D_MODEL), F32)
    cs_s[CONV_HDR:, :] = _dot(u, wmqk_ref[...])
    tail_ref[...] = cs_s[rows:rows + CONV_HDR, :]
    _conv_qk(cs_s, convw_ref, convb_ref, q_s, k_s, 0, rows, valid)

    vt_all = _dot_nt(wmvt_ref[...], u)
    _store_values_ext(jnp.where(valid_t, vt_all, 0.0), vt_s)
    _store_gate_forms(0, _gate_rows(vt_all[ML_HEADS * ML_DV:, :], gbias_ref, valid_t), gr_s, br_s, cb_s)

    c0_ref[...] = jnp.zeros(c0_ref.shape, F32)
    m0_ref[...] = jnp.zeros(m0_ref.shape, F32)
    _mlstm_chunk_all_heads(0, gr_s, br_s, cb_s, q_s, k_s, vt_s, c0_ref, m0_ref, None)


def _attn_kernel(x_ref, gpre_ref, wqt_ref, wk_ref, wvt_ref, wz_ref, wga_ref, wao_ref,
                 rc_ref, rs1_ref, rs2_ref, rct_ref, rst_ref, km2_ref, vmt_ref, bt_ref, bt0_ref, eye_ref, sink_ref,
                 out_ref,
                 u_s, q_s, kk_s, vt_s, a_s, gated_s, zg_s, ga_s):
    j = pl.program_id(1)
    tile = x_ref.shape[1]
    nblk = tile // ATTN_BLOCK
    pairs = N_HEADS // 2

    @pl.when(j == 0)
    def _():
        kk_s[:, 0:ATTN_BLOCK, :] = jnp.zeros((N_KV, ATTN_BLOCK, LANES), BF16)
        vt_s[:, 0] = jnp.zeros((N_KV, HEAD_DIM, ATTN_BLOCK), BF16)

    u_s[...] = _rmsnorm(x_ref[0], gpre_ref[...]).astype(BF16)

    half_r = ROPE_DIM // 2
    cos_t, sin_t = rct_ref[...], rst_ref[...]
    zero_blk = jnp.zeros((HEAD_DIM, ATTN_BLOCK), BF16)
    qt_all = _dot_nt(wqt_ref[...], u_s[...])
    for c in range(pairs):
        qt = qt_all[2 * HEAD_DIM * c:2 * HEAD_DIM * (c + 1), :]
        parts_t = []
        for hh in range(2):
            base = HEAD_DIM * hh
            x1, x2 = qt[base:base + half_r], qt[base + half_r:base + ROPE_DIM]
            parts_t += [x1 * cos_t - x2 * sin_t, x2 * cos_t + x1 * sin_t, qt[base + ROPE_DIM:base + HEAD_DIM]]
        qb = (jnp.concatenate(parts_t, axis=0) * (HEAD_DIM ** -0.5)).astype(BF16)
        for i in range(nblk):
            blk = qb[:, ATTN_BLOCK * i:ATTN_BLOCK * (i + 1)]
            q_s[c, i] = jnp.concatenate(
                [jnp.concatenate([blk[:HEAD_DIM], zero_blk], axis=1),
                 jnp.concatenate([zero_blk, blk[HEAD_DIM:]], axis=1)], axis=0)

    c_t, s1, s2 = rc_ref[...], rs1_ref[...], rs2_ref[...]
    low = lax.broadcasted_iota(jnp.int32, (tile, LANES), 1) < HEAD_DIM

    kc = _dot(u_s[...], wk_ref[...])
    for cb in range(N_KV // 2):
        blk = _rope(kc[:, LANES * cb:LANES * (cb + 1)], c_t, s1, s2)
        swapped = pltpu.roll(blk, HEAD_DIM, 1)
        kk_s[2 * cb, ATTN_BLOCK:, :] = jnp.where(low, blk, swapped).astype(BF16)
        kk_s[2 * cb + 1, ATTN_BLOCK:, :] = jnp.where(low, swapped, blk).astype(BF16)
    vt_all = _dot_nt(wvt_ref[...], u_s[...])
    for g in range(N_KV):
        for bi in range(nblk):
            vt_s[g, 1 + bi] = vt_all[HEAD_DIM * g:HEAD_DIM * (g + 1),
                                     ATTN_BLOCK * bi:ATTN_BLOCK * (bi + 1)].astype(BF16)

    def block_scores(i):
        r0 = i * ATTN_BLOCK
        mask = jnp.where(j == 0, bt0_ref[...], bt_ref[...]) if i == 0 else bt_ref[...]
        out = []
        for c in range(pairs):
            g = c // 2
            keys = jnp.concatenate([kk_s[g, r0:r0 + 2 * ATTN_BLOCK, :], km2_ref[g]], axis=0)
            band = jnp.concatenate([keys, mask], axis=1)
            rhs = jnp.concatenate([q_s[c, i], eye_ref[...]], axis=0)
            out.append(_dot(band, rhs))
        return out

    ones_rows = jnp.where(lax.broadcasted_iota(jnp.int32, (BF16_ROWS, BAND), 0) == 0, 1.0, 0.0).astype(BF16)
    always = pl.program_id(0) >= 0
    def gate_slice(i, anchors=None):
        k = 0
        for r in range(tile // ROW_GROUP):
            rows = slice(ROW_GROUP * r, ROW_GROUP * (r + 1))
            for w_ref, dst, act in ((wz_ref, zg_s, _silu), (wga_ref, ga_s, _sigmoid)):
                lhs = u_s[rows, :]
                if anchors is not None:
                    head = jnp.where(always, lhs[0:BF16_ROWS, 0:LANES], anchors[k])
                    lhs = jnp.concatenate(
                        [jnp.concatenate([head, lhs[0:BF16_ROWS, LANES:]], axis=1), lhs[BF16_ROWS:]], axis=0)
                dst[i, rows, :] = act(_dot(lhs, w_ref[i]))
                k += 1

    scores = block_scores(0)
    for i in range(nblk):
        r0 = i * ATTN_BLOCK
        next_scores = block_scores(i + 1) if i + 1 < nblk else None
        probs = []
        for c in range(pairs):
            pbs, sink_terms = [], []
            for hh in range(2):
                lanes = slice(ATTN_BLOCK * hh, ATTN_BLOCK * (hh + 1))
                st = scores[c][:, lanes]
                sink = sink_ref[c, 0:1, lanes]
                mx = jnp.maximum(jnp.max(st, axis=0, keepdims=True), sink)
                pbs.append(jnp.exp(st - mx).astype(BF16))
                sink_terms.append(jnp.exp(sink - mx))
            probs.append((jnp.concatenate(pbs, axis=1), jnp.concatenate(sink_terms, axis=1)))
        gate_slice(i, [probs[2 * k][0][0:BF16_ROWS, 0:LANES] for k in range(4)])
        for c in range(pairs):
            g = c // 2
            pb, sink_term = probs[c]
            vt = jnp.concatenate([vt_s[g, i], vt_s[g, i + 1], vmt_ref[g][:, 0:N_META]], axis=1)
            ot = _dot(jnp.concatenate([vt, ones_rows], axis=0), pb)
            ot = ot[0:HEAD_DIM] * (1.0 / (ot[HEAD_DIM:HEAD_DIM + 1] + sink_term))
            o = jnp.concatenate([ot[:, :ATTN_BLOCK], ot[:, ATTN_BLOCK:]], axis=0).T
            a_s[c, r0:r0 + ATTN_BLOCK, :] = o
        scores = next_scores

    kk_s[:, 0:ATTN_BLOCK, :] = kk_s[:, tile:tile + ATTN_BLOCK, :]
    vt_s[:, 0] = vt_s[:, nblk]

    width = zg_s.shape[2]
    per = width // LANES
    for g4 in range(nblk):
        att = jnp.concatenate([a_s[per * g4 + p] for p in range(per)], axis=1)
        gated_s[:, width * g4:width * (g4 + 1)] = (att * zg_s[g4]).astype(BF16)
    for g4 in range(nblk):
        cols = slice(width * g4, width * (g4 + 1))
        out_ref[0, :, cols] = (ga_s[g4] * _dot(gated_s[...], wao_ref[:, cols])).astype(out_ref.dtype)


def _mlstm_kernel(x_ref, ya_ref, gpre_ref, gpost_ref, wmqk_ref, wmvt_ref, wmo_ref, wmz_ref,
                  wgm_ref, wmlo_ref, wout_ref, convw_ref, convb_ref, gbias_ref, hnorm_ref,
                  tail0_ref, c0_ref, m0_ref,
                  out_ref,
                  u_s, cs_s, q_s, k_s, vt_s, gr_s, br_s, cb_s, hs_s, hm_s, mg_s, c_s, m_s,
                  og_s, mz_s, gm_s):
    j = pl.program_id(1)
    tile = x_ref.shape[1]

    @pl.when(j == 0)
    def _():
        cs_s[0:CONV_HDR, :] = tail0_ref[...]
        c_s[...] = c0_ref[...]
        m_s[...] = m0_ref[...]

    u_s[...] = _rmsnorm(x_ref[0], gpre_ref[...]).astype(BF16)

    vt_all = _dot_nt(wmvt_ref[...], u_s[...])
    for c in range(tile // CHUNK):
        graw = vt_all[ML_HEADS * ML_DV:, CHUNK * c:CHUNK * (c + 1)]
        _store_gate_forms(c, _gate_rows(graw, gbias_ref), gr_s, br_s, cb_s)
    _store_values_ext(vt_all, vt_s)
    hw = D_MODEL // 2
    for half in range(2):
        cols = slice(hw * half, hw * (half + 1))
        cs_s[CONV_HDR:, cols] = _dot(u_s[...], wmqk_ref[:, cols])

    for c in range(tile // CHUNK):
        def gate_slice():
            wcat = jnp.concatenate([wmo_ref[c], wmz_ref[c], wgm_ref[c]], axis=1)
            gates = _dot(u_s[...], wcat)
            og_s[c] = _sigmoid(gates[:, 0:ML_DV])
            mz_s[c] = _silu(gates[:, ML_DV:2 * ML_DV])
            gm_s[c] = _sigmoid(gates[:, 2 * ML_DV:])

        _conv_qk(cs_s, convw_ref, convb_ref, q_s, k_s, CHUNK * c, CHUNK)
        _mlstm_chunk_all_heads(c, gr_s, br_s, cb_s, q_s, k_s, vt_s, c_s, m_s, hs_s, gate_slice)
    cs_s[0:CONV_HDR, :] = cs_s[tile:tile + CONV_HDR, :]

    for r in range(tile // ROW_GROUP):
        rows = slice(ROW_GROUP * r, ROW_GROUP * (r + 1))
        for h in range(ML_HEADS):
            cols = slice(ML_DV * h, ML_DV * (h + 1))
            hg = og_s[h, rows, :] * hs_s[rows, cols]
            hc = hg - jnp.mean(hg, axis=-1, keepdims=True)
            y = hc * lax.rsqrt(jnp.mean(hc * hc, axis=-1, keepdims=True) + LN_EPS) * hnorm_ref[:, cols]
            hm_s[rows, cols] = (y * mz_s[h, rows, :]).astype(BF16)
        for h in range(ML_HEADS):
            cols = slice(ML_DV * h, ML_DV * (h + 1))
            ym = _dot(hm_s[rows, :], wmlo_ref[:, cols])
            mg_s[rows, cols] = (ya_ref[0, rows, cols].astype(F32) + gm_s[h, rows, :] * ym).astype(BF16)
        o = _dot(mg_s[rows, :], wout_ref[...])
        out_ref[0, rows, :] = x_ref[0, rows, :] + _rmsnorm(o, gpost_ref[...])


def _rope_tables(pos):
    half = ROPE_DIM // 2
    inv_freq = ROPE_THETA ** (-jnp.arange(0, ROPE_DIM, 2, dtype=F32) / ROPE_DIM)
    ang = pos[:, None] * inv_freq[None, :]
    cos, sin = jnp.cos(ang), jnp.sin(ang)
    ones = jnp.ones((pos.shape[0], HEAD_DIM - ROPE_DIM), F32)
    zeros_h = jnp.zeros((pos.shape[0], half), F32)
    zeros_r = jnp.zeros_like(ones)
    c_head = jnp.concatenate([cos, cos, ones], axis=1)
    s1_head = jnp.concatenate([-sin, zeros_h, zeros_r], axis=1)
    s2_head = jnp.concatenate([zeros_h, sin, zeros_r], axis=1)
    rep = LANES // HEAD_DIM
    return (jnp.tile(c_head, (1, rep)), jnp.tile(s1_head, (1, rep)), jnp.tile(s2_head, (1, rep)))


def _band_bias():
    kj = np.arange(2 * ATTN_BLOCK)[:, None]
    qi = np.arange(ATTN_BLOCK)[None, :]
    vis = (kj > qi) & (kj <= qi + ATTN_BLOCK)
    meta = np.zeros((N_META, ATTN_BLOCK), np.float32)
    bt = np.concatenate([np.where(vis, 0.0, NEG_BIG), meta], axis=0).astype(np.float32)
    bt0 = np.concatenate([np.where(vis & (kj >= ATTN_BLOCK), 0.0, NEG_BIG), meta], axis=0).astype(np.float32)
    eye2 = np.tile(np.eye(ATTN_BLOCK, dtype=np.float32), (1, 2))
    return jnp.asarray(bt, BF16), jnp.asarray(bt0, BF16), jnp.asarray(eye2, BF16)


def _full(shape, single=True):
    nd = len(shape)
    kw = {"pipeline_mode": pl.Buffered(1)} if single else {}
    return pl.BlockSpec(shape, lambda *_: (0,) * nd, **kw)


def kernel(x, meta_tokens, norm_pre, w_in, attn_sinks, conv_w, conv_b, mlstm_gate_bias, mlstm_head_norm,
           w_attn_out, w_mlstm_out, w_out, norm_post):
    B, S, D = x.shape
    assert D == D_MODEL and w_in.shape[0] == 1 and meta_tokens.shape == (N_META, D_MODEL)
    tile = min(SEQ_TILE, S)
    assert S % tile == 0 and tile % CHUNK == 0 and tile % ATTN_BLOCK == 0
    nt = S // tile
    nblk = tile // ATTN_BLOCK
    nchunk = tile // CHUNK

    w = w_in[0].astype(BF16)
    o = 0
    parts = {}
    for name, width in (("q", 1024), ("k", 256), ("v", 256), ("z", 1024), ("mqk", 1024), ("mv", 1024),
                        ("mi", 4), ("mf", 4), ("mo", 1024), ("mz", 1024), ("ga", 1024), ("gm", 1024)):
        parts[name] = w[:, o:o + width]
        o += width
    nslice = tile // ATTN_BLOCK
    assert nslice == nchunk == ML_HEADS and D // nslice == ML_DV

    def sliced(wm):
        return wm.reshape(D, nslice, wm.shape[1] // nslice).transpose(1, 0, 2)

    wvt = parts["v"].T
    wqt = parts["q"].T
    gate_rows = BF16_ROWS
    wmvt = jnp.concatenate([parts["mv"].T, parts["mi"].T, parts["mf"].T,
                            jnp.zeros((gate_rows - 2 * ML_HEADS, D), BF16)], axis=0)
    gbias = jnp.broadcast_to(jnp.concatenate([mlstm_gate_bias[0].astype(F32),
                                              jnp.zeros((gate_rows - 2 * ML_HEADS,), F32)])[:, None],
                             (gate_rows, CHUNK))
    gpre = norm_pre[0].astype(F32)[None, :]
    gpost = norm_post[0].astype(F32)[None, :]
    convw = conv_w[0].astype(F32)
    convb = conv_b[0].astype(F32)[None, :]
    hnorm = mlstm_head_norm[0].astype(F32)[None, :]
    wao = w_attn_out[0].astype(BF16)
    wmlo = w_mlstm_out[0].astype(BF16)
    wout = w_out[0].astype(BF16)
    sinks = attn_sinks[0].astype(F32)
    sink_rows = jnp.broadcast_to(jnp.repeat(sinks.reshape(N_HEADS // 2, 2), ATTN_BLOCK, axis=1)[:, None, :],
                                 (N_HEADS // 2, 8, 2 * ATTN_BLOCK))

    xm = jnp.concatenate([jnp.zeros((CHUNK - N_META, D), F32), meta_tokens.astype(F32)], axis=0)
    mpos = jnp.maximum(jnp.arange(CHUNK, dtype=F32) - (CHUNK - N_META), 0.0)
    mrc, mrs1, mrs2 = _rope_tables(mpos)
    kv_w = N_KV * HEAD_DIM
    meta_out = pl.pallas_call(
        _meta_kernel,
        out_shape=(jax.ShapeDtypeStruct((CHUNK, kv_w), F32), jax.ShapeDtypeStruct((CHUNK, kv_w), F32),
                   jax.ShapeDtypeStruct((CONV_HDR, D), F32),
                   jax.ShapeDtypeStruct((ML_HEADS // 2, ML_EXT, 2 * ML_DK), F32),
                   jax.ShapeDtypeStruct((8, LANES), F32)),
        scratch_shapes=[pltpu.VMEM((CONV_HDR + CHUNK, D), F32), pltpu.VMEM((CHUNK, D // 2), BF16),
                        pltpu.VMEM((CHUNK, D // 2), BF16), pltpu.VMEM((ML_HEADS // 2, 1, ML_EXT, 2 * CHUNK), F32),
                        pltpu.VMEM((1, 8, CHUNK), F32), pltpu.VMEM((1, 8, CHUNK), F32),
                        pltpu.VMEM((1, CHUNK, LANES), F32)],
        compiler_params=pltpu.CompilerParams(vmem_limit_bytes=VMEM_LIMIT),
        name="meta_tokens",
    )(xm, gpre, parts["k"], parts["v"], parts["mqk"], wmvt, convw, convb, gbias,
      mrc, mrs1, mrs2)
    kmeta, vmeta, tail0, c0, m0 = meta_out
    km = kmeta[CHUNK - N_META:].reshape(N_META, N_KV, HEAD_DIM).transpose(1, 0, 2)
    km2 = jnp.concatenate([km, km], axis=2).astype(BF16)
    vm = vmeta[CHUNK - N_META:].reshape(N_META, N_KV, HEAD_DIM).transpose(1, 2, 0)
    vmt = jnp.concatenate([vm, jnp.zeros((N_KV, HEAD_DIM, LANES - N_META), F32)], axis=2).astype(BF16)

    rc, rs1, rs2 = _rope_tables(jnp.arange(S, dtype=F32) + N_META)
    rct, rst = rc[:, 0:ROPE_DIM // 2].T, rs2[:, ROPE_DIM // 2:ROPE_DIM].T
    bt, bt0, eye2 = _band_bias()
    x_spec = pl.BlockSpec((1, tile, D), lambda b, j: (b, j, 0))
    rope_spec = pl.BlockSpec((tile, LANES), lambda b, j: (j, 0))
    rope_t_spec = pl.BlockSpec((ROPE_DIM // 2, tile), lambda b, j: (0, j))
    seq_params = pltpu.CompilerParams(dimension_semantics=("arbitrary", "arbitrary"),
                                      vmem_limit_bytes=VMEM_LIMIT)
    ya = pl.pallas_call(
        _attn_kernel,
        grid=(B, nt),
        in_specs=[x_spec, _full((1, D)),
                  _full((D, 1024)), _full((D, kv_w)), _full((kv_w, D)), _full((nslice, D, ML_DV)),
                  _full((nslice, D, ML_DV)), _full((D, D)),
                  rope_spec, rope_spec, rope_spec, rope_t_spec, rope_t_spec,
                  _full((N_KV, N_META, LANES)), _full((N_KV, HEAD_DIM, LANES)),
                  _full((BAND, ATTN_BLOCK)), _full((BAND, ATTN_BLOCK)), _full((ATTN_BLOCK, 2 * ATTN_BLOCK)),
                  _full((N_HEADS // 2, 8, 2 * ATTN_BLOCK))],
        out_specs=pl.BlockSpec((1, tile, D), lambda b, j: (b, j, 0)),
        out_shape=jax.ShapeDtypeStruct((B, S, D), BF16),
        scratch_shapes=[pltpu.VMEM((tile, D), BF16),
                        pltpu.VMEM((N_HEADS // 2, nblk, 2 * HEAD_DIM, 2 * ATTN_BLOCK), BF16),
                        pltpu.VMEM((N_KV, ATTN_BLOCK + tile, LANES), BF16),
                        pltpu.VMEM((N_KV, nblk + 1, HEAD_DIM, ATTN_BLOCK), BF16),
                        pltpu.VMEM((N_HEADS // 2, tile, LANES), F32),
                        pltpu.VMEM((tile, D), BF16),
                        pltpu.VMEM((nslice, tile, ML_DV), F32),
                        pltpu.VMEM((nslice, tile, ML_DV), F32)],
        compiler_params=seq_params,
        name="attn_branch",
    )(x, gpre, wqt, parts["k"], wvt, sliced(parts["z"]), sliced(parts["ga"]), wao,
      rc, rs1, rs2, rct, rst, km2, vmt, bt, bt0, eye2, sink_rows)

    out = pl.pallas_call(
        _mlstm_kernel,
        grid=(B, nt),
        in_specs=[x_spec, pl.BlockSpec((1, tile, D), lambda b, j: (b, j, 0)),
                  _full((1, D)), _full((1, D)),
                  _full((D, 1024)), _full((1024 + gate_rows, D)), _full((nslice, D, ML_DV)),
                  _full((nslice, D, ML_DV)), _full((nslice, D, ML_DV)), _full((D, D)), _full((D, D)),
                  _full((CONV_W, D)), _full((1, D)), _full((gate_rows, CHUNK)), _full((1, D)),
                  _full((CONV_HDR, D)), _full((ML_HEADS // 2, ML_EXT, 2 * ML_DK)), _full((8, LANES))],
        out_specs=pl.BlockSpec((1, tile, D), lambda b, j: (b, j, 0)),
        out_shape=jax.ShapeDtypeStruct((B, S, D), x.dtype),
        scratch_shapes=[pltpu.VMEM((tile, D), BF16),
                        pltpu.VMEM((CONV_HDR + tile, D), F32),
                        pltpu.VMEM((tile, D // 2), BF16),
                        pltpu.VMEM((tile, D // 2), BF16),
                        pltpu.VMEM((ML_HEADS // 2, nchunk, ML_EXT, 2 * CHUNK), F32),
                        pltpu.VMEM((nchunk, 8, CHUNK), F32),
                        pltpu.VMEM((nchunk, 8, CHUNK), F32),
                        pltpu.VMEM((nchunk, CHUNK, LANES), F32),
                        pltpu.VMEM((tile, D), F32),
                        pltpu.VMEM((tile, D), BF16),
                        pltpu.VMEM((tile, D), BF16),
                        pltpu.VMEM((ML_HEADS // 2, ML_EXT, 2 * ML_DK), F32),
                        pltpu.VMEM((8, LANES), F32),
                        pltpu.VMEM((nslice, tile, ML_DV), F32),
                        pltpu.VMEM((nslice, tile, ML_DV), F32),
                        pltpu.VMEM((nslice, tile, ML_DV), F32)],
        compiler_params=seq_params,
        name="mlstm_merge_out",
    )(x, ya, gpre, gpost, parts["mqk"], wmvt, sliced(parts["mo"]), sliced(parts["mz"]),
      sliced(parts["gm"]),
      wmlo, wout, convw, convb, gbias, hnorm, tail0, c0, m0)
    return out
```

```python
import numpy as np
import jax
import jax.numpy as jnp
from jax import lax
from jax.experimental import pallas as pl
from jax.experimental.pallas import tpu as pltpu

F32 = jnp.float32
BF16 = jnp.bfloat16

D_MODEL = 1024
N_META = 16
HEAD_DIM = 64
N_HEADS = 16
N_KV = 4
ATTN_BLOCK = 128
BAND = N_META + 2 * ATTN_BLOCK
ROPE_DIM = 16
ROPE_THETA = 500000.0
ML_HEADS = 4
ML_DV = 256
ML_DK = 128
BF16_ROWS = 16
ML_EXT = ML_DV + BF16_ROWS
CONV_W = 4
CHUNK = 128
RMS_EPS = 1e-6
LN_EPS = 1e-6
NEG_BIG = -1e30
LANES = 128
CONV_HDR = 8
CONV_COLS = 256
SEQ_TILE = 512
ROW_GROUP = 256
VMEM_LIMIT = 56 * 1024 * 1024


def _dot(a, b):
    return jnp.dot(a, b, preferred_element_type=F32)


def _dot_nt(a, b):
    return lax.dot_general(a, b, (((1,), (1,)), ((), ())), preferred_element_type=F32)


def _sigmoid(x):
    return 0.5 * jnp.tanh(0.5 * x) + 0.5


def _silu(x):
    return x * _sigmoid(x)


def _log_sigmoid(x):
    return jnp.minimum(x, 0.0) - jnp.log(1.0 + jnp.exp(-jnp.abs(x)))


def _rmsnorm(xf, g):
    ms = jnp.mean(xf * xf, axis=-1, keepdims=True)
    return xf * lax.rsqrt(ms + RMS_EPS) * g


def _rope(blk, c_t, s1, s2):
    return blk * c_t + pltpu.roll(blk, LANES - 8, 1) * s1 + pltpu.roll(blk, 8, 1) * s2


def _causal_conv_silu(cs_ref, w_ref, b_ref, row0, rows, cols):
    acc = b_ref[:, cols]
    for tap in range(CONV_W):
        off = CONV_HDR - (CONV_W - 1) + tap + row0
        acc = acc + w_ref[tap:tap + 1, cols] * cs_ref[off:off + rows, cols]
    return _silu(acc)


def _conv_qk(cs_ref, w_ref, b_ref, q_ref, k_ref, row0, rows, valid=None):
    half = D_MODEL // 2
    for cc in range(D_MODEL // CONV_COLS):
        cols = slice(CONV_COLS * cc, CONV_COLS * (cc + 1))
        act = _causal_conv_silu(cs_ref, w_ref, b_ref, row0, rows, cols)
        if valid is not None:
            act = jnp.where(valid, act, 0.0)
        if CONV_COLS * cc < half:
            q_ref[row0:row0 + rows, cols] = act.astype(BF16)
        else:
            k_ref[row0:row0 + rows, CONV_COLS * cc - half:CONV_COLS * (cc + 1) - half] = (
                act * (ML_DK ** -0.5)).astype(BF16)


def _gate_rows(graw, gbias_ref, valid_t=None):
    g = (graw + gbias_ref[...])[0:8, :]
    sub = lax.broadcasted_iota(jnp.int32, g.shape, 0)
    grow = jnp.where(sub < ML_HEADS, g, _log_sigmoid(g))
    if valid_t is not None:
        grow = jnp.where(valid_t, grow, jnp.where(sub < ML_HEADS, NEG_BIG, 0.0))
    return grow


def _store_gate_forms(c, grow, gr_ref, br_ref, cb_ref):
    lane = lax.broadcasted_iota(jnp.int32, grow.shape, 1)
    brow = grow
    sh = 1
    while sh < CHUNK:
        brow = brow + jnp.where(lane >= sh, pltpu.roll(brow, sh, 1), 0.0)
        sh *= 2
    diff = brow[ML_HEADS:2 * ML_HEADS, :] - grow[0:ML_HEADS, :]
    pad = jnp.zeros((LANES - ML_HEADS, CHUNK), F32)
    gr_ref[c], br_ref[c], cb_ref[c] = grow, brow, jnp.concatenate([diff, pad], axis=0).T


def _mlstm_chunk_all_heads(c, gr_ref, br_ref, cb_ref, q_ref, k_ref, vt_ref, c_ref, m_ref, hs_ref,
                           between=None):
    r0 = c * CHUNK if isinstance(c, int) else pl.multiple_of(c * CHUNK, CHUNK)
    n = CHUNK
    grow, brow, cbm = gr_ref[c], br_ref[c], cb_ref[c]
    row = lax.broadcasted_iota(jnp.int32, (n, 2 * n), 0)
    col = lax.broadcasted_iota(jnp.int32, (n, 2 * n), 1)
    tri2 = row <= (col & (n - 1))
    lo_row = lax.broadcasted_iota(jnp.int32, (1, 2 * n), 1) < n
    lo = lax.broadcasted_iota(jnp.int32, (n, 2 * ML_DK), 1) < ML_DK

    def block_diag(x2):
        zero = jnp.zeros_like(x2)
        return jnp.concatenate([jnp.where(lo, x2, zero), jnp.where(lo, zero, x2)], axis=0)

    staged = []
    for p in range(ML_HEADS // 2):
        a, b = 2 * p, 2 * p + 1
        cols = slice(2 * ML_DK * p, 2 * ML_DK * (p + 1))
        q2, k2, vt2, c2 = q_ref[pl.ds(r0, n), cols], k_ref[pl.ds(r0, n), cols], vt_ref[p, c], c_ref[p]
        li2 = jnp.concatenate([grow[a:a + 1, :], grow[b:b + 1, :]], axis=1)
        b2 = jnp.concatenate([brow[ML_HEADS + a:ML_HEADS + a + 1, :],
                              brow[ML_HEADS + b:ML_HEADS + b + 1, :]], axis=1)
        m_a, m_b = m_ref[a:a + 1, 0:1], m_ref[b:b + 1, 0:1]
        m_prev2 = jnp.where(lo_row, m_a, m_b)
        bl_a, bl_b = b2[:, n - 1:n], b2[:, 2 * n - 1:2 * n]
        b_last2 = jnp.where(lo_row, bl_a, bl_b)
        sc = _dot_nt(jnp.concatenate([k2, c2.astype(BF16)], axis=0), block_diag(q2))
        a_row2 = b_last2 - b2 + li2
        mn_a = jnp.maximum(bl_a + m_a, jnp.max(a_row2[:, :n], axis=1, keepdims=True))
        mn_b = jnp.maximum(bl_b + m_b, jnp.max(a_row2[:, n:], axis=1, keepdims=True))
        m_new2 = jnp.where(lo_row, mn_a, mn_b)
        w_c2 = jnp.exp(b_last2 + m_prev2 - m_new2)
        w_a2 = jnp.exp(a_row2 - m_new2)
        c_ref[p] = w_c2 * c2 + _dot((vt2 * w_a2).astype(BF16), block_diag(k2))
        m_ref[a:a + 1, :] = jnp.broadcast_to(mn_a, (1, LANES))
        m_ref[b:b + 1, :] = jnp.broadcast_to(mn_b, (1, LANES))
        staged.append((sc, vt2, b2, m_prev2))
    if between is not None:
        between()
    if hs_ref is None:
        return
    for p, (sc, vt2, b2, m_prev2) in enumerate(staged):
        a, b = 2 * p, 2 * p + 1
        cb2 = jnp.concatenate([jnp.broadcast_to(cbm[:, a:a + 1], (n, n)),
                               jnp.broadcast_to(cbm[:, b:b + 1], (n, n))], axis=1)
        dt = jnp.where(tri2, b2 - cb2, NEG_BIG)
        inter2 = b2 + m_prev2
        m_row2 = jnp.maximum(inter2, jnp.max(dt, axis=0, keepdims=True))
        sct2 = (sc[0:n] * jnp.exp(dt - m_row2)).astype(BF16)
        het = jnp.exp(inter2 - m_row2) * sc[n:] + _dot(vt2.astype(BF16), block_diag(sct2))
        den = het[ML_DV:ML_DV + 1, :]
        ht2 = het[0:ML_DV, :] / jnp.maximum(jnp.abs(den), jnp.exp(-m_row2))
        hs_ref[pl.ds(r0, n), ML_DV * a:ML_DV * (a + 1)] = ht2[:, :n].T
        hs_ref[pl.ds(r0, n), ML_DV * b:ML_DV * (b + 1)] = ht2[:, n:].T


def _store_values_ext(vt_all, vt_ref):
    nchunk = vt_all.shape[1] // CHUNK
    sub = lax.broadcasted_iota(jnp.int32, (ML_EXT - ML_DV, 2 * CHUNK), 0)
    tail = jnp.where(sub == 0, 1.0, 0.0)
    for p in range(ML_HEADS // 2):
        for c in range(nchunk):
            tok = slice(CHUNK * c, CHUNK * (c + 1))
            for hh in range(2):
                h = 2 * p + hh
                vt_ref[p, c, 0:ML_DV, CHUNK * hh:CHUNK * (hh + 1)] = vt_all[ML_DV * h:ML_DV * (h + 1), tok]
            vt_ref[p, c, ML_DV:ML_EXT, :] = tail


def _meta_kernel(xm_ref, gpre_ref, wk_ref, wv_ref, wmqk_ref, wmvt_ref,
                 convw_ref, convb_ref, gbias_ref, rc_ref, rs1_ref, rs2_ref,
                 kmeta_ref, vmeta_ref, tail_ref, c0_ref, m0_ref,
                 cs_s, q_s, k_s, vt_s, gr_s, br_s, cb_s):
    rows = xm_ref.shape[0]
    u = _rmsnorm(xm_ref[...], gpre_ref[...]).astype(BF16)
    valid = lax.broadcasted_iota(jnp.int32, (rows, 1), 0) >= rows - N_META
    valid_t = lax.broadcasted_iota(jnp.int32, (1, rows), 1) >= rows - N_META

    kc = _dot(u, wk_ref[...])
    for cb in range(kc.shape[1] // LANES):
        sl = slice(LANES * cb, LANES * (cb + 1))
        kmeta_ref[:, sl] = _rope(kc[:, sl], rc_ref[...], rs1_ref[...], rs2_ref[...])
    vmeta_ref[...] = _dot(u, wv_ref[...])

    cs_s[0:CONV_HDR, :] = jnp.zeros((CONV_HDR, D_MODEL), F32)
    cs_s[CONV_HDR:, :] = _dot(u, wmqk_ref[...])
    tail_ref[...] = cs_s[rows:rows + CONV_HDR, :]
    _conv_qk(cs_s, convw_ref, convb_ref, q_s, k_s, 0, rows, valid)

    vt_all = _dot_nt(wmvt_ref[...], u)
    _store_values_ext(jnp.where(valid_t, vt_all, 0.0), vt_s)
    _store_gate_forms(0, _gate_rows(vt_all[ML_HEADS * ML_DV:, :], gbias_ref, valid_t), gr_s, br_s, cb_s)

    c0_ref[...] = jnp.zeros(c0_ref.shape, F32)
    m0_ref[...] = jnp.zeros(m0_ref.shape, F32)
    _mlstm_chunk_all_heads(0, gr_s, br_s, cb_s, q_s, k_s, vt_s, c0_ref, m0_ref, None)


def _attn_kernel(x_ref, *refs):
    kk_s, vt_s = refs[-6], refs[-5]
    tile = refs[-8].shape[0]
    j = pl.program_id(1)

    @pl.when(j == 0)
    def _():
        kk_s[:, 0:ATTN_BLOCK, :] = jnp.zeros((N_KV, ATTN_BLOCK, LANES), BF16)
        vt_s[:, 0] = jnp.zeros((N_KV, HEAD_DIM, ATTN_BLOCK), BF16)

    for t in range(x_ref.shape[1] // tile):
        _attn_tile(t, j, x_ref, *refs)


def _attn_tile(t, j, x_ref, gpre_ref, wqt_ref, wk_ref, wvt_ref, wz_ref, wga_ref, wao_ref,
               rc_ref, rs1_ref, rs2_ref, rct_ref, rst_ref, km2_ref, vmt_ref, bt_ref, bt0_ref, eye_ref, sink_ref,
               out_ref,
               u_s, q_s, kk_s, vt_s, a_s, gated_s, zg_s, ga_s):
    tile = u_s.shape[0]
    nblk = tile // ATTN_BLOCK
    pairs = N_HEADS // 2
    rows_t = slice(tile * t, tile * (t + 1))

    u_s[...] = _rmsnorm(x_ref[0, rows_t, :], gpre_ref[...]).astype(BF16)

    half_r = ROPE_DIM // 2
    cos_t, sin_t = rct_ref[:, rows_t], rst_ref[:, rows_t]
    zero_blk = jnp.zeros((HEAD_DIM, ATTN_BLOCK), BF16)
    qt_all = _dot_nt(wqt_ref[...], u_s[...])
    for c in range(pairs):
        qt = qt_all[2 * HEAD_DIM * c:2 * HEAD_DIM * (c + 1), :]
        parts_t = []
        for hh in range(2):
            base = HEAD_DIM * hh
            x1, x2 = qt[base:base + half_r], qt[base + half_r:base + ROPE_DIM]
            parts_t += [x1 * cos_t - x2 * sin_t, x2 * cos_t + x1 * sin_t, qt[base + ROPE_DIM:base + HEAD_DIM]]
        qb = (jnp.concatenate(parts_t, axis=0) * (HEAD_DIM ** -0.5)).astype(BF16)
        for i in range(nblk):
            blk = qb[:, ATTN_BLOCK * i:ATTN_BLOCK * (i + 1)]
            q_s[c, i] = jnp.concatenate(
                [jnp.concatenate([blk[:HEAD_DIM], zero_blk], axis=1),
                 jnp.concatenate([zero_blk, blk[HEAD_DIM:]], axis=1)], axis=0)

    c_t, s1, s2 = rc_ref[rows_t, :], rs1_ref[rows_t, :], rs2_ref[rows_t, :]
    low = lax.broadcasted_iota(jnp.int32, (tile, LANES), 1) < HEAD_DIM

    kc = _dot(u_s[...], wk_ref[...])
    for cb in range(N_KV // 2):
        blk = _rope(kc[:, LANES * cb:LANES * (cb + 1)], c_t, s1, s2)
        swapped = pltpu.roll(blk, HEAD_DIM, 1)
        kk_s[2 * cb, ATTN_BLOCK:, :] = jnp.where(low, blk, swapped).astype(BF16)
        kk_s[2 * cb + 1, ATTN_BLOCK:, :] = jnp.where(low, swapped, blk).astype(BF16)
    vt_all = _dot_nt(wvt_ref[...], u_s[...])
    for g in range(N_KV):
        for bi in range(nblk):
            vt_s[g, 1 + bi] = vt_all[HEAD_DIM * g:HEAD_DIM * (g + 1),
                                     ATTN_BLOCK * bi:ATTN_BLOCK * (bi + 1)].astype(BF16)

    def block_scores(i):
        r0 = i * ATTN_BLOCK
        mask = jnp.where(j == 0, bt0_ref[...], bt_ref[...]) if (i == 0 and t == 0) else bt_ref[...]
        out = []
        for c in range(pairs):
            g = c // 2
            keys = jnp.concatenate([kk_s[g, r0:r0 + 2 * ATTN_BLOCK, :], km2_ref[g]], axis=0)
            band = jnp.concatenate([keys, mask], axis=1)
            rhs = jnp.concatenate([q_s[c, i], eye_ref[...]], axis=0)
            out.append(_dot(band, rhs))
        return out

    ones_rows = jnp.where(lax.broadcasted_iota(jnp.int32, (BF16_ROWS, BAND), 0) == 0, 1.0, 0.0).astype(BF16)
    always = pl.program_id(0) >= 0
    def gate_slice(i, anchors=None):
        k = 0
        for r in range(tile // ROW_GROUP):
            rows = slice(ROW_GROUP * r, ROW_GROUP * (r + 1))
            for w_ref, dst, act in ((wz_ref, zg_s, _silu), (wga_ref, ga_s, _sigmoid)):
                lhs = u_s[rows, :]
                if anchors is not None:
                    head = jnp.where(always, lhs[0:BF16_ROWS, 0:LANES], anchors[k])
                    lhs = jnp.concatenate(
                        [jnp.concatenate([head, lhs[0:BF16_ROWS, LANES:]], axis=1), lhs[BF16_ROWS:]], axis=0)
                dst[i, rows, :] = act(_dot(lhs, w_ref[i]))
                k += 1

    scores = block_scores(0)
    for i in range(nblk):
        r0 = i * ATTN_BLOCK
        next_scores = block_scores(i + 1) if i + 1 < nblk else None
        probs = []
        for c in range(pairs):
            pbs, sink_terms = [], []
            for hh in range(2):
                lanes = slice(ATTN_BLOCK * hh, ATTN_BLOCK * (hh + 1))
                st = scores[c][:, lanes]
                sink = sink_ref[c, 0:1, lanes]
                mx = jnp.maximum(jnp.max(st, axis=0, keepdims=True), sink)
                pbs.append(jnp.exp(st - mx).astype(BF16))
                sink_terms.append(jnp.exp(sink - mx))
            probs.append((jnp.concatenate(pbs, axis=1), jnp.concatenate(sink_terms, axis=1)))
        gate_slice(i, [probs[2 * k][0][0:BF16_ROWS, 0:LANES] for k in range(4)])
        for c in range(pairs):
            g = c // 2
            pb, sink_term = probs[c]
            vt = jnp.concatenate([vt_s[g, i], vt_s[g, i + 1], vmt_ref[g][:, 0:N_META]], axis=1)
            ot = _dot(jnp.concatenate([vt, ones_rows], axis=0), pb)
            ot = ot[0:HEAD_DIM] * (1.0 / (ot[HEAD_DIM:HEAD_DIM + 1] + sink_term))
            o = jnp.concatenate([ot[:, :ATTN_BLOCK], ot[:, ATTN_BLOCK:]], axis=0).T
            a_s[c, r0:r0 + ATTN_BLOCK, :] = o
        scores = next_scores

    kk_s[:, 0:ATTN_BLOCK, :] = kk_s[:, tile:tile + ATTN_BLOCK, :]
    vt_s[:, 0] = vt_s[:, nblk]

    width = zg_s.shape[2]
    per = width // LANES
    for g4 in range(nblk):
        att = jnp.concatenate([a_s[per * g4 + p] for p in range(per)], axis=1)
        gated_s[:, width * g4:width * (g4 + 1)] = (att * zg_s[g4]).astype(BF16)
    for g4 in range(nblk):
        cols = slice(width * g4, width * (g4 + 1))
        out_ref[0, rows_t, cols] = (ga_s[g4] * _dot(gated_s[...], wao_ref[:, cols])).astype(out_ref.dtype)


def _mlstm_kernel(x_ref, ya_ref, gpre_ref, gpost_ref, wmqk_ref, wmvt_ref, wmo_ref, wmz_ref,
                  wgm_ref, wmlo_ref, wout_ref, convw_ref, convb_ref, gbias_ref, hnorm_ref,
                  tail0_ref, c0_ref, m0_ref,
                  out_ref,
                  u_s, cs_s, q_s, k_s, vt_s, gr_s, br_s, cb_s, hs_s, hm_s, mg_s, c_s, m_s,
                  og_s, mz_s, gm_s):
    j = pl.program_id(1)
    tile = x_ref.shape[1]

    @pl.when(j == 0)
    def _():
        cs_s[0:CONV_HDR, :] = tail0_ref[...]
        c_s[...] = c0_ref[...]
        m_s[...] = m0_ref[...]

    u_s[...] = _rmsnorm(x_ref[0], gpre_ref[...]).astype(BF16)

    vt_all = _dot_nt(wmvt_ref[...], u_s[...])
    for c in range(tile // CHUNK):
        graw = vt_all[ML_HEADS * ML_DV:, CHUNK * c:CHUNK * (c + 1)]
        _store_gate_forms(c, _gate_rows(graw, gbias_ref), gr_s, br_s, cb_s)
    _store_values_ext(vt_all, vt_s)
    hw = D_MODEL // 2
    for half in range(2):
        cols = slice(hw * half, hw * (half + 1))
        cs_s[CONV_HDR:, cols] = _dot(u_s[...], wmqk_ref[:, cols])

    for c in range(tile // CHUNK):
        def gate_slice():
            wcat = jnp.concatenate([wmo_ref[c], wmz_ref[c], wgm_ref[c]], axis=1)
            gates = _dot(u_s[...], wcat)
            og_s[c] = _sigmoid(gates[:, 0:ML_DV])
            mz_s[c] = _silu(gates[:, ML_DV:2 * ML_DV])
            gm_s[c] = _sigmoid(gates[:, 2 * ML_DV:])

        _conv_qk(cs_s, convw_ref, convb_ref, q_s, k_s, CHUNK * c, CHUNK)
        _mlstm_chunk_all_heads(c, gr_s, br_s, cb_s, q_s, k_s, vt_s, c_s, m_s, hs_s, gate_slice)
    cs_s[0:CONV_HDR, :] = cs_s[tile:tile + CONV_HDR, :]

    for r in range(tile // ROW_GROUP):
        rows = slice(ROW_GROUP * r, ROW_GROUP * (r + 1))
        for h in range(ML_HEADS):
            cols = slice(ML_DV * h, ML_DV * (h + 1))
            hg = og_s[h, rows, :] * hs_s[rows, cols]
            hc = hg - jnp.mean(hg, axis=-1, keepdims=True)
            y = hc * lax.rsqrt(jnp.mean(hc * hc, axis=-1, keepdims=True) + LN_EPS) * hnorm_ref[:, cols]
            hm_s[rows, cols] = (y * mz_s[h, rows, :]).astype(BF16)
        for h in range(ML_HEADS):
            cols = slice(ML_DV * h, ML_DV * (h + 1))
            ym = _dot(hm_s[rows, :], wmlo_ref[:, cols])
            mg_s[rows, cols] = (ya_ref[0, rows, cols].astype(F32) + gm_s[h, rows, :] * ym).astype(BF16)
        o = _dot(mg_s[rows, :], wout_ref[...])
        out_ref[0, rows, :] = x_ref[0, rows, :] + _rmsnorm(o, gpost_ref[...])


def _rope_tables(pos):
    half = ROPE_DIM // 2
    inv_freq = ROPE_THETA ** (-jnp.arange(0, ROPE_DIM, 2, dtype=F32) / ROPE_DIM)
    ang = pos[:, None] * inv_freq[None, :]
    cos, sin = jnp.cos(ang), jnp.sin(ang)
    ones = jnp.ones((pos.shape[0], HEAD_DIM - ROPE_DIM), F32)
    zeros_h = jnp.zeros((pos.shape[0], half), F32)
    zeros_r = jnp.zeros_like(ones)
    c_head = jnp.concatenate([cos, cos, ones], axis=1)
    s1_head = jnp.concatenate([-sin, zeros_h, zeros_r], axis=1)
    s2_head = jnp.concatenate([zeros_h, sin, zeros_r], axis=1)
    rep = LANES // HEAD_DIM
    return (jnp.tile(c_head, (1, rep)), jnp.tile(s1_head, (1, rep)), jnp.tile(s2_head, (1, rep)))


def _band_bias():
    kj = np.arange(2 * ATTN_BLOCK)[:, None]
    qi = np.arange(ATTN_BLOCK)[None, :]
    vis = (kj > qi) & (kj <= qi + ATTN_BLOCK)
    meta = np.zeros((N_META, ATTN_BLOCK), np.float32)
    bt = np.concatenate([np.where(vis, 0.0, NEG_BIG), meta], axis=0).astype(np.float32)
    bt0 = np.concatenate([np.where(vis & (kj >= ATTN_BLOCK), 0.0, NEG_BIG), meta], axis=0).astype(np.float32)
    eye2 = np.tile(np.eye(ATTN_BLOCK, dtype=np.float32), (1, 2))
    return jnp.asarray(bt, BF16), jnp.asarray(bt0, BF16), jnp.asarray(eye2, BF16)


def _full(shape, single=True):
    nd = len(shape)
    kw = {"pipeline_mode": pl.Buffered(1)} if single else {}
    return pl.BlockSpec(shape, lambda *_: (0,) * nd, **kw)


def kernel(x, meta_tokens, norm_pre, w_in, attn_sinks, conv_w, conv_b, mlstm_gate_bias, mlstm_head_norm,
           w_attn_out, w_mlstm_out, w_out, norm_post):
    B, S, D = x.shape
    assert D == D_MODEL and w_in.shape[0] == 1 and meta_tokens.shape == (N_META, D_MODEL)
    tile = min(SEQ_TILE, S)
    assert S % tile == 0 and tile % CHUNK == 0 and tile % ATTN_BLOCK == 0
    nt = S // tile
    nblk = tile // ATTN_BLOCK
    nchunk = tile // CHUNK

    w = w_in[0].astype(BF16)
    o = 0
    parts = {}
    for name, width in (("q", 1024), ("k", 256), ("v", 256), ("z", 1024), ("mqk", 1024), ("mv", 1024),
                        ("mi", 4), ("mf", 4), ("mo", 1024), ("mz", 1024), ("ga", 1024), ("gm", 1024)):
        parts[name] = w[:, o:o + width]
        o += width
    nslice = tile // ATTN_BLOCK
    assert nslice == nchunk == ML_HEADS and D // nslice == ML_DV

    def sliced(wm):
        return wm.reshape(D, nslice, wm.shape[1] // nslice).transpose(1, 0, 2)

    wvt = parts["v"].T
    wqt = parts["q"].T
    gate_rows = BF16_ROWS
    wmvt = jnp.concatenate([parts["mv"].T, parts["mi"].T, parts["mf"].T,
                            jnp.zeros((gate_rows - 2 * ML_HEADS, D), BF16)], axis=0)
    gbias = jnp.broadcast_to(jnp.concatenate([mlstm_gate_bias[0].astype(F32),
                                              jnp.zeros((gate_rows - 2 * ML_HEADS,), F32)])[:, None],
                             (gate_rows, CHUNK))
    gpre = norm_pre[0].astype(F32)[None, :]
    gpost = norm_post[0].astype(F32)[None, :]
    convw = conv_w[0].astype(F32)
    convb = conv_b[0].astype(F32)[None, :]
    hnorm = mlstm_head_norm[0].astype(F32)[None, :]
    wao = w_attn_out[0].astype(BF16)
    wmlo = w_mlstm_out[0].astype(BF16)
    wout = w_out[0].astype(BF16)
    sinks = attn_sinks[0].astype(F32)
    sink_rows = jnp.broadcast_to(jnp.repeat(sinks.reshape(N_HEADS // 2, 2), ATTN_BLOCK, axis=1)[:, None, :],
                                 (N_HEADS // 2, 8, 2 * ATTN_BLOCK))

    xm = jnp.concatenate([jnp.zeros((CHUNK - N_META, D), F32), meta_tokens.astype(F32)], axis=0)
    mpos = jnp.maximum(jnp.arange(CHUNK, dtype=F32) - (CHUNK - N_META), 0.0)
    mrc, mrs1, mrs2 = _rope_tables(mpos)
    kv_w = N_KV * HEAD_DIM
    meta_out = pl.pallas_call(
        _meta_kernel,
        out_shape=(jax.ShapeDtypeStruct((CHUNK, kv_w), F32), jax.ShapeDtypeStruct((CHUNK, kv_w), F32),
                   jax.ShapeDtypeStruct((CONV_HDR, D), F32),
                   jax.ShapeDtypeStruct((ML_HEADS // 2, ML_EXT, 2 * ML_DK), F32),
                   jax.ShapeDtypeStruct((8, LANES), F32)),
        scratch_shapes=[pltpu.VMEM((CONV_HDR + CHUNK, D), F32), pltpu.VMEM((CHUNK, D // 2), BF16),
                        pltpu.VMEM((CHUNK, D // 2), BF16), pltpu.VMEM((ML_HEADS // 2, 1, ML_EXT, 2 * CHUNK), F32),
                        pltpu.VMEM((1, 8, CHUNK), F32), pltpu.VMEM((1, 8, CHUNK), F32),
                        pltpu.VMEM((1, CHUNK, LANES), F32)],
        compiler_params=pltpu.CompilerParams(vmem_limit_bytes=VMEM_LIMIT),
        name="meta_tokens",
    )(xm, gpre, parts["k"], parts["v"], parts["mqk"], wmvt, convw, convb, gbias,
      mrc, mrs1, mrs2)
    kmeta, vmeta, tail0, c0, m0 = meta_out
    km = kmeta[CHUNK - N_META:].reshape(N_META, N_KV, HEAD_DIM).transpose(1, 0, 2)
    km2 = jnp.concatenate([km, km], axis=2).astype(BF16)
    vm = vmeta[CHUNK - N_META:].reshape(N_META, N_KV, HEAD_DIM).transpose(1, 2, 0)
    vmt = jnp.concatenate([vm, jnp.zeros((N_KV, HEAD_DIM, LANES - N_META), F32)], axis=2).astype(BF16)

    rc, rs1, rs2 = _rope_tables(jnp.arange(S, dtype=F32) + N_META)
    rct, rst = rc[:, 0:ROPE_DIM // 2].T, rs2[:, ROPE_DIM // 2:ROPE_DIM].T
    bt, bt0, eye2 = _band_bias()
    x_spec = pl.BlockSpec((1, tile, D), lambda b, j: (b, j, 0))
    seq_params = pltpu.CompilerParams(dimension_semantics=("arbitrary", "arbitrary"),
                                      vmem_limit_bytes=VMEM_LIMIT)
    assert nt % 2 == 0
    x2_spec = pl.BlockSpec((1, 2 * tile, D), lambda b, j: (b, j, 0))
    rope_spec = pl.BlockSpec((2 * tile, LANES), lambda b, j: (j, 0))
    rope_t_spec = pl.BlockSpec((ROPE_DIM // 2, 2 * tile), lambda b, j: (0, j))
    ya = pl.pallas_call(
        _attn_kernel,
        grid=(B, nt // 2),
        in_specs=[x2_spec, _full((1, D)),
                  _full((D, 1024)), _full((D, kv_w)), _full((kv_w, D)), _full((nslice, D, ML_DV)),
                  _full((nslice, D, ML_DV)), _full((D, D)),
                  rope_spec, rope_spec, rope_spec, rope_t_spec, rope_t_spec,
                  _full((N_KV, N_META, LANES)), _full((N_KV, HEAD_DIM, LANES)),
                  _full((BAND, ATTN_BLOCK)), _full((BAND, ATTN_BLOCK)), _full((ATTN_BLOCK, 2 * ATTN_BLOCK)),
                  _full((N_HEADS // 2, 8, 2 * ATTN_BLOCK))],
        out_specs=pl.BlockSpec((1, 2 * tile, D), lambda b, j: (b, j, 0)),
        out_shape=jax.ShapeDtypeStruct((B, S, D), BF16),
        scratch_shapes=[pltpu.VMEM((tile, D), BF16),
                        pltpu.VMEM((N_HEADS // 2, nblk, 2 * HEAD_DIM, 2 * ATTN_BLOCK), BF16),
                        pltpu.VMEM((N_KV, ATTN_BLOCK + tile, LANES), BF16),
                        pltpu.VMEM((N_KV, nblk + 1, HEAD_DIM, ATTN_BLOCK), BF16),
                        pltpu.VMEM((N_HEADS // 2, tile, LANES), F32),
                        pltpu.VMEM((tile, D), BF16),
                        pltpu.VMEM((nslice, tile, ML_DV), F32),
                        pltpu.VMEM((nslice, tile, ML_DV), F32)],
        compiler_params=seq_params,
        name="attn_branch",
    )(x, gpre, wqt, parts["k"], wvt, sliced(parts["z"]), sliced(parts["ga"]), wao,
      rc, rs1, rs2, rct, rst, km2, vmt, bt, bt0, eye2, sink_rows)

    out = pl.pallas_call(
        _mlstm_kernel,
        grid=(B, nt),
        in_specs=[x_spec, pl.BlockSpec((1, tile, D), lambda b, j: (b, j, 0)),
                  _full((1, D)), _full((1, D)),
                  _full((D, 1024)), _full((1024 + gate_rows, D)), _full((nslice, D, ML_DV)),
                  _full((nslice, D, ML_DV)), _full((nslice, D, ML_DV)), _full((D, D)), _full((D, D)),
                  _full((CONV_W, D)), _full((1, D)), _full((gate_rows, CHUNK)), _full((1, D)),
                  _full((CONV_HDR, D)), _full((ML_HEADS // 2, ML_EXT, 2 * ML_DK)), _full((8, LANES))],
        out_specs=pl.BlockSpec((1, tile, D), lambda b, j: (b, j, 0)),
        out_shape=jax.ShapeDtypeStruct((B, S, D), x.dtype),
        scratch_shapes=[pltpu.VMEM((tile, D), BF16),
                        pltpu.VMEM((CONV_HDR + tile, D), F32),
                        pltpu.VMEM((tile, D // 2), BF16),
                        pltpu.VMEM((tile, D // 2), BF16),
                        pltpu.VMEM((ML_HEADS // 2, nchunk, ML_EXT, 2 * CHUNK), F32),
                        pltpu.VMEM((nchunk, 8, CHUNK), F32),
                        pltpu.VMEM((nchunk, 8, CHUNK), F32),
                        pltpu.VMEM((nchunk, CHUNK, LANES), F32),
                        pltpu.VMEM((tile, D), F32),
                        pltpu.VMEM((tile, D), BF16),
                        pltpu.VMEM((tile, D), BF16),
                        pltpu.VMEM((ML_HEADS // 2, ML_EXT, 2 * ML_DK), F32),
                        pltpu.VMEM((8, LANES), F32),
                        pltpu.VMEM((nslice, tile, ML_DV), F32),
                        pltpu.VMEM((nslice, tile, ML_DV), F32),
                        pltpu.VMEM((nslice, tile, ML_DV), F32)],
        compiler_params=seq_params,
        name="mlstm_merge_out",
    )(x, ya, gpre, gpost, parts["mqk"], wmvt, sliced(parts["mo"]), sliced(parts["mz"]),
      sliced(parts["gm"]),
      wmlo, wout, convw, convb, gbias, hnorm, tail0, c0, m0)
    return out
```

```python
import numpy as np
import jax
import jax.numpy as jnp
from jax import lax
from jax.experimental import pallas as pl
from jax.experimental.pallas import tpu as pltpu

F32 = jnp.float32
BF16 = jnp.bfloat16

D_MODEL = 1024
N_META = 16
HEAD_DIM = 64
N_HEADS = 16
N_KV = 4
ATTN_BLOCK = 128
BAND = N_META + 2 * ATTN_BLOCK
ROPE_DIM = 16
ROPE_THETA = 500000.0
ML_HEADS = 4
ML_DV = 256
ML_DK = 128
BF16_ROWS = 16
ML_EXT = ML_DV + BF16_ROWS
CONV_W = 4
CHUNK = 128
RMS_EPS = 1e-6
LN_EPS = 1e-6
NEG_BIG = -1e30
LANES = 128
CONV_HDR = 8
CONV_COLS = 256
SEQ_TILE = 512
ROW_GROUP = 256
VMEM_LIMIT = 56 * 1024 * 1024


def _dot(a, b):
    return jnp.dot(a, b, preferred_element_type=F32)


def _dot_nt(a, b):
    return lax.dot_general(a, b, (((1,), (1,)), ((), ())), preferred_element_type=F32)


def _sigmoid(x):
    return 0.5 * jnp.tanh(0.5 * x) + 0.5


def _silu(x):
    return x * _sigmoid(x)


def _log_sigmoid(x):
    return jnp.minimum(x, 0.0) - jnp.log(1.0 + jnp.exp(-jnp.abs(x)))


def _rmsnorm(xf, g):
    ms = jnp.mean(xf * xf, axis=-1, keepdims=True)
    return xf * lax.rsqrt(ms + RMS_EPS) * g


def _rope(blk, c_t, s1, s2):
    return blk * c_t + pltpu.roll(blk, LANES - 8, 1) * s1 + pltpu.roll(blk, 8, 1) * s2


def _causal_conv_silu(cs_ref, w_ref, b_ref, row0, rows, cols):
    acc = b_ref[:, cols]
    for tap in range(CONV_W):
        off = CONV_HDR - (CONV_W - 1) + tap + row0
        acc = acc + w_ref[tap:tap + 1, cols] * cs_ref[off:off + rows, cols]
    return _silu(acc)


def _conv_qk(cs_ref, w_ref, b_ref, q_ref, k_ref, row0, rows, valid=None):
    half = D_MODEL // 2
    for cc in range(D_MODEL // CONV_COLS):
        cols = slice(CONV_COLS * cc, CONV_COLS * (cc + 1))
        act = _causal_conv_silu(cs_ref, w_ref, b_ref, row0, rows, cols)
        if valid is not None:
            act = jnp.where(valid, act, 0.0)
        if CONV_COLS * cc < half:
            q_ref[row0:row0 + rows, cols] = act.astype(BF16)
        else:
            k_ref[row0:row0 + rows, CONV_COLS * cc - half:CONV_COLS * (cc + 1) - half] = (
                act * (ML_DK ** -0.5)).astype(BF16)


def _gate_rows(graw, gbias_ref, valid_t=None):
    g = (graw + gbias_ref[...])[0:8, :]
    sub = lax.broadcasted_iota(jnp.int32, g.shape, 0)
    grow = jnp.where(sub < ML_HEADS, g, _log_sigmoid(g))
    if valid_t is not None:
        grow = jnp.where(valid_t, grow, jnp.where(sub < ML_HEADS, NEG_BIG, 0.0))
    return grow


def _store_gate_forms(c, grow, gr_ref, br_ref, cb_ref):
    lane = lax.broadcasted_iota(jnp.int32, grow.shape, 1)
    brow = grow
    sh = 1
    while sh < CHUNK:
        brow = brow + jnp.where(lane >= sh, pltpu.roll(brow, sh, 1), 0.0)
        sh *= 2
    diff = brow[ML_HEADS:2 * ML_HEADS, :] - grow[0:ML_HEADS, :]
    pad = jnp.zeros((LANES - ML_HEADS, CHUNK), F32)
    gr_ref[c], br_ref[c], cb_ref[c] = grow, brow, jnp.concatenate([diff, pad], axis=0).T


def _mlstm_chunk_all_heads(c, gr_ref, br_ref, cb_ref, q_ref, k_ref, vt_ref, c_ref, m_ref, hs_ref,
                           between=None):
    r0 = c * CHUNK if isinstance(c, int) else pl.multiple_of(c * CHUNK, CHUNK)
    n = CHUNK
    grow, brow, cbm = gr_ref[c], br_ref[c], cb_ref[c]
    row = lax.broadcasted_iota(jnp.int32, (n, 2 * n), 0)
    col = lax.broadcasted_iota(jnp.int32, (n, 2 * n), 1)
    tri2 = row <= (col & (n - 1))
    lo_row = lax.broadcasted_iota(jnp.int32, (1, 2 * n), 1) < n
    lo = lax.broadcasted_iota(jnp.int32, (n, 2 * ML_DK), 1) < ML_DK

    def block_diag(x2):
        zero = jnp.zeros_like(x2)
        return jnp.concatenate([jnp.where(lo, x2, zero), jnp.where(lo, zero, x2)], axis=0)

    staged = []
    for p in range(ML_HEADS // 2):
        a, b = 2 * p, 2 * p + 1
        cols = slice(2 * ML_DK * p, 2 * ML_DK * (p + 1))
        q2, k2, vt2, c2 = q_ref[pl.ds(r0, n), cols], k_ref[pl.ds(r0, n), cols], vt_ref[p, c], c_ref[p]
        li2 = jnp.concatenate([grow[a:a + 1, :], grow[b:b + 1, :]], axis=1)
        b2 = jnp.concatenate([brow[ML_HEADS + a:ML_HEADS + a + 1, :],
                              brow[ML_HEADS + b:ML_HEADS + b + 1, :]], axis=1)
        m_a, m_b = m_ref[a:a + 1, 0:1], m_ref[b:b + 1, 0:1]
        m_prev2 = jnp.where(lo_row, m_a, m_b)
        bl_a, bl_b = b2[:, n - 1:n], b2[:, 2 * n - 1:2 * n]
        b_last2 = jnp.where(lo_row, bl_a, bl_b)
        sc = _dot_nt(jnp.concatenate([k2, c2.astype(BF16)], axis=0), block_diag(q2))
        a_row2 = b_last2 - b2 + li2
        mn_a = jnp.maximum(bl_a + m_a, jnp.max(a_row2[:, :n], axis=1, keepdims=True))
        mn_b = jnp.maximum(bl_b + m_b, jnp.max(a_row2[:, n:], axis=1, keepdims=True))
        m_new2 = jnp.where(lo_row, mn_a, mn_b)
        w_c2 = jnp.exp(b_last2 + m_prev2 - m_new2)
        w_a2 = jnp.exp(a_row2 - m_new2)
        c_ref[p] = w_c2 * c2 + _dot((vt2 * w_a2).astype(BF16), block_diag(k2))
        m_ref[a:a + 1, :] = jnp.broadcast_to(mn_a, (1, LANES))
        m_ref[b:b + 1, :] = jnp.broadcast_to(mn_b, (1, LANES))
        staged.append((sc, vt2, b2, m_prev2))
    if between is not None:
        between()
    if hs_ref is None:
        return
    for p, (sc, vt2, b2, m_prev2) in enumerate(staged):
        a, b = 2 * p, 2 * p + 1
        cb2 = jnp.concatenate([jnp.broadcast_to(cbm[:, a:a + 1], (n, n)),
                               jnp.broadcast_to(cbm[:, b:b + 1], (n, n))], axis=1)
        dt = jnp.where(tri2, b2 - cb2, NEG_BIG)
        inter2 = b2 + m_prev2
        m_row2 = jnp.maximum(inter2, jnp.max(dt, axis=0, keepdims=True))
        sct2 = (sc[0:n] * jnp.exp(dt - m_row2)).astype(BF16)
        het = jnp.exp(inter2 - m_row2) * sc[n:] + _dot(vt2.astype(BF16), block_diag(sct2))
        den = het[ML_DV:ML_DV + 1, :]
        ht2 = het[0:ML_DV, :] / jnp.maximum(jnp.abs(den), jnp.exp(-m_row2))
        hs_ref[pl.ds(r0, n), ML_DV * a:ML_DV * (a + 1)] = ht2[:, :n].T
        hs_ref[pl.ds(r0, n), ML_DV * b:ML_DV * (b + 1)] = ht2[:, n:].T


def _store_values_ext(vt_all, vt_ref):
    nchunk = vt_all.shape[1] // CHUNK
    sub = lax.broadcasted_iota(jnp.int32, (ML_EXT - ML_DV, 2 * CHUNK), 0)
    tail = jnp.where(sub == 0, 1.0, 0.0)
    for p in range(ML_HEADS // 2):
        for c in range(nchunk):
            tok = slice(CHUNK * c, CHUNK * (c + 1))
            for hh in range(2):
                h = 2 * p + hh
                vt_ref[p, c, 0:ML_DV, CHUNK * hh:CHUNK * (hh + 1)] = vt_all[ML_DV * h:ML_DV * (h + 1), tok]
            vt_ref[p, c, ML_DV:ML_EXT, :] = tail


def _meta_kernel(xm_ref, gpre_ref, wk_ref, wv_ref, wmqk_ref, wmvt_ref,
                 convw_ref, convb_ref, gbias_ref, rc_ref, rs1_ref, rs2_ref,
                 kmeta_ref, vmeta_ref, tail_ref, c0_ref, m0_ref,
                 cs_s, q_s, k_s, vt_s, gr_s, br_s, cb_s):
    rows = xm_ref.shape[0]
    u = _rmsnorm(xm_ref[...], gpre_ref[...]).astype(BF16)
    valid = lax.broadcasted_iota(jnp.int32, (rows, 1), 0) >= rows - N_META
    valid_t = lax.broadcasted_iota(jnp.int32, (1, rows), 1) >= rows - N_META

    kc = _dot(u, wk_ref[...])
    for cb in range(kc.shape[1] // LANES):
        sl = slice(LANES * cb, LANES * (cb + 1))
        kmeta_ref[:, sl] = _rope(kc[:, sl], rc_ref[...], rs1_ref[...], rs2_ref[...])
    vmeta_ref[...] = _dot(u, wv_ref[...])

    cs_s[0:CONV_HDR, :] = jnp.zeros((CONV_HDR, D_MODEL), F32)
    cs_s[CONV_HDR:, :] = _dot(u, wmqk_ref[...])
    tail_ref[...] = cs_s[rows:rows + CONV_HDR, :]
    _conv_qk(cs_s, convw_ref, convb_ref, q_s, k_s, 0, rows, valid)

    vt_all = _dot_nt(wmvt_ref[...], u)
    _store_values_ext(jnp.where(valid_t, vt_all, 0.0), vt_s)
    _store_gate_forms(0, _gate_rows(vt_all[ML_HEADS * ML_DV:, :], gbias_ref, valid_t), gr_s, br_s, cb_s)

    c0_ref[...] = jnp.zeros(c0_ref.shape, F32)
    m0_ref[...] = jnp.zeros(m0_ref.shape, F32)
    _mlstm_chunk_all_heads(0, gr_s, br_s, cb_s, q_s, k_s, vt_s, c0_ref, m0_ref, None)


def _attn_kernel(x_ref, *refs):
    kk_s, vt_s = refs[-6], refs[-5]
    tile = refs[-8].shape[0]
    j = pl.program_id(1)

    @pl.when(j == 0)
    def _():
        kk_s[:, 0:ATTN_BLOCK, :] = jnp.zeros((N_KV, ATTN_BLOCK, LANES), BF16)
        vt_s[:, 0] = jnp.zeros((N_KV, HEAD_DIM, ATTN_BLOCK), BF16)

    for t in range(x_ref.shape[1] // tile):
        _attn_tile(t, j, x_ref, *refs)


def _attn_tile(t, j, x_ref, gpre_ref, wqt_ref, wk_ref, wvt_ref, wz_ref, wga_ref, wao_ref,
               rc_ref, rs1_ref, rs2_ref, rct_ref, rst_ref, km2_ref, vmt_ref, bt_ref, bt0_ref, eye_ref, sink_ref,
               out_ref,
               u_s, q_s, kk_s, vt_s, a_s, gated_s, zg_s, ga_s):
    tile = u_s.shape[0]
    nblk = tile // ATTN_BLOCK
    pairs = N_HEADS // 2
    rows_t = slice(tile * t, tile * (t + 1))

    u_s[...] = _rmsnorm(x_ref[0, rows_t, :], gpre_ref[...]).astype(BF16)

    half_r = ROPE_DIM // 2
    cos_t, sin_t = rct_ref[:, rows_t], rst_ref[:, rows_t]
    zero_blk = jnp.zeros((HEAD_DIM, ATTN_BLOCK), BF16)
    qt_all = _dot_nt(wqt_ref[...], u_s[...])
    for c in range(pairs):
        qt = qt_all[2 * HEAD_DIM * c:2 * HEAD_DIM * (c + 1), :]
        parts_t = []
        for hh in range(2):
            base = HEAD_DIM * hh
            x1, x2 = qt[base:base + half_r], qt[base + half_r:base + ROPE_DIM]
            parts_t += [x1 * cos_t - x2 * sin_t, x2 * cos_t + x1 * sin_t, qt[base + ROPE_DIM:base + HEAD_DIM]]
        qb = (jnp.concatenate(parts_t, axis=0) * (HEAD_DIM ** -0.5)).astype(BF16)
        for i in range(nblk):
            blk = qb[:, ATTN_BLOCK * i:ATTN_BLOCK * (i + 1)]
            q_s[c, i] = jnp.concatenate(
                [jnp.concatenate([blk[:HEAD_DIM], zero_blk], axis=1),
                 jnp.concatenate([zero_blk, blk[HEAD_DIM:]], axis=1)], axis=0)

    c_t, s1, s2 = rc_ref[rows_t, :], rs1_ref[rows_t, :], rs2_ref[rows_t, :]
    low = lax.broadcasted_iota(jnp.int32, (tile, LANES), 1) < HEAD_DIM

    kc = _dot(u_s[...], wk_ref[...])
    for cb in range(N_KV // 2):
        blk = _rope(kc[:, LANES * cb:LANES * (cb + 1)], c_t, s1, s2)
        swapped = pltpu.roll(blk, HEAD_DIM, 1)
        kk_s[2 * cb, ATTN_BLOCK:, :] = jnp.where(low, blk, swapped).astype(BF16)
        kk_s[2 * cb + 1, ATTN_BLOCK:, :] = jnp.where(low, swapped, blk).astype(BF16)
    vt_all = _dot_nt(wvt_ref[...], u_s[...])
    for g in range(N_KV):
        for bi in range(nblk):
            vt_s[g, 1 + bi] = vt_all[HEAD_DIM * g:HEAD_DIM * (g + 1),
                                     ATTN_BLOCK * bi:ATTN_BLOCK * (bi + 1)].astype(BF16)

    def block_scores(i):
        r0 = i * ATTN_BLOCK
        mask = jnp.where(j == 0, bt0_ref[...], bt_ref[...]) if (i == 0 and t == 0) else bt_ref[...]
        out = []
        for c in range(pairs):
            g = c // 2
            keys = jnp.concatenate([kk_s[g, r0:r0 + 2 * ATTN_BLOCK, :], km2_ref[g]], axis=0)
            band = jnp.concatenate([keys, mask], axis=1)
            rhs = jnp.concatenate([q_s[c, i], eye_ref[...]], axis=0)
            out.append(_dot(band, rhs))
        return out

    ones_rows = jnp.where(lax.broadcasted_iota(jnp.int32, (BF16_ROWS, BAND), 0) == 0, 1.0, 0.0).astype(BF16)
    always = pl.program_id(0) >= 0
    def gate_slice(i, anchors=None):
        k = 0
        for r in range(tile // ROW_GROUP):
            rows = slice(ROW_GROUP * r, ROW_GROUP * (r + 1))
            for w_ref, dst, act in ((wz_ref, zg_s, _silu), (wga_ref, ga_s, _sigmoid)):
                lhs = u_s[rows, :]
                if anchors is not None:
                    head = jnp.where(always, lhs[0:BF16_ROWS, 0:LANES], anchors[k])
                    lhs = jnp.concatenate(
                        [jnp.concatenate([head, lhs[0:BF16_ROWS, LANES:]], axis=1), lhs[BF16_ROWS:]], axis=0)
                dst[i, rows, :] = act(_dot(lhs, w_ref[i])).astype(dst.dtype)
                k += 1

    scores = block_scores(0)
    for i in range(nblk):
        r0 = i * ATTN_BLOCK
        next_scores = block_scores(i + 1) if i + 1 < nblk else None
        probs = []
        for c in range(pairs):
            pbs, sink_terms = [], []
            for hh in range(2):
                lanes = slice(ATTN_BLOCK * hh, ATTN_BLOCK * (hh + 1))
                st = scores[c][:, lanes]
                sink = sink_ref[c, 0:1, lanes]
                mx = jnp.maximum(jnp.max(st, axis=0, keepdims=True), sink)
                pbs.append(jnp.exp(st - mx).astype(BF16))
                sink_terms.append(jnp.exp(sink - mx))
            probs.append((jnp.concatenate(pbs, axis=1), jnp.concatenate(sink_terms, axis=1)))
        gate_slice(i, [probs[2 * k][0][0:BF16_ROWS, 0:LANES] for k in range(4)])
        for c in range(pairs):
            g = c // 2
            pb, sink_term = probs[c]
            vt = jnp.concatenate([vt_s[g, i], vt_s[g, i + 1], vmt_ref[g][:, 0:N_META]], axis=1)
            ot = _dot(jnp.concatenate([vt, ones_rows], axis=0), pb)
            ot = ot[0:HEAD_DIM] * (1.0 / (ot[HEAD_DIM:HEAD_DIM + 1] + sink_term))
            o = jnp.concatenate([ot[:, :ATTN_BLOCK], ot[:, ATTN_BLOCK:]], axis=0).T
            a_s[c, r0:r0 + ATTN_BLOCK, :] = o
        scores = next_scores

    kk_s[:, 0:ATTN_BLOCK, :] = kk_s[:, tile:tile + ATTN_BLOCK, :]
    vt_s[:, 0] = vt_s[:, nblk]

    width = zg_s.shape[2]
    per = width // LANES
    for g4 in range(nblk):
        att = jnp.concatenate([a_s[per * g4 + p] for p in range(per)], axis=1)
        gated_s[:, width * g4:width * (g4 + 1)] = (att * zg_s[g4]).astype(BF16)
    for g4 in range(nblk):
        cols = slice(width * g4, width * (g4 + 1))
        out_ref[0, rows_t, cols] = (ga_s[g4] * _dot(gated_s[...], wao_ref[:, cols])).astype(out_ref.dtype)


def _mlstm_kernel(x_ref, ya_ref, gpre_ref, gpost_ref, wmqk_ref, wmvt_ref, wmo_ref, wmz_ref,
                  wgm_ref, wmlo_ref, wout_ref, convw_ref, convb_ref, gbias_ref, hnorm_ref,
                  tail0_ref, c0_ref, m0_ref,
                  out_ref,
                  u_s, cs_s, q_s, k_s, vt_s, gr_s, br_s, cb_s, hs_s, hm_s, mg_s, c_s, m_s,
                  og_s, mz_s, gm_s):
    j = pl.program_id(1)
    tile = x_ref.shape[1]

    @pl.when(j == 0)
    def _():
        cs_s[0:CONV_HDR, :] = tail0_ref[...]
        c_s[...] = c0_ref[...]
        m_s[...] = m0_ref[...]

    u_s[...] = _rmsnorm(x_ref[0], gpre_ref[...]).astype(BF16)

    vt_all = _dot_nt(wmvt_ref[...], u_s[...])
    for c in range(tile // CHUNK):
        graw = vt_all[ML_HEADS * ML_DV:, CHUNK * c:CHUNK * (c + 1)]
        _store_gate_forms(c, _gate_rows(graw, gbias_ref), gr_s, br_s, cb_s)
    _store_values_ext(vt_all, vt_s)
    hw = D_MODEL // 2
    for half in range(2):
        cols = slice(hw * half, hw * (half + 1))
        cs_s[CONV_HDR:, cols] = _dot(u_s[...], wmqk_ref[:, cols])

    for c in range(tile // CHUNK):
        def gate_slice():
            wcat = jnp.concatenate([wmo_ref[c], wmz_ref[c], wgm_ref[c]], axis=1)
            gates = _dot(u_s[...], wcat)
            og_s[c] = _sigmoid(gates[:, 0:ML_DV])
            mz_s[c] = _silu(gates[:, ML_DV:2 * ML_DV])
            gm_s[c] = _sigmoid(gates[:, 2 * ML_DV:])

        _conv_qk(cs_s, convw_ref, convb_ref, q_s, k_s, CHUNK * c, CHUNK)
        _mlstm_chunk_all_heads(c, gr_s, br_s, cb_s, q_s, k_s, vt_s, c_s, m_s, hs_s, gate_slice)
    cs_s[0:CONV_HDR, :] = cs_s[tile:tile + CONV_HDR, :]

    for r in range(tile // ROW_GROUP):
        rows = slice(ROW_GROUP * r, ROW_GROUP * (r + 1))
        for h in range(ML_HEADS):
            cols = slice(ML_DV * h, ML_DV * (h + 1))
            hg = og_s[h, rows, :] * hs_s[rows, cols]
            hc = hg - jnp.mean(hg, axis=-1, keepdims=True)
            y = hc * lax.rsqrt(jnp.mean(hc * hc, axis=-1, keepdims=True) + LN_EPS) * hnorm_ref[:, cols]
            hm_s[rows, cols] = (y * mz_s[h, rows, :]).astype(BF16)
        for h in range(ML_HEADS):
            cols = slice(ML_DV * h, ML_DV * (h + 1))
            ym = _dot(hm_s[rows, :], wmlo_ref[:, cols])
            mg_s[rows, cols] = (ya_ref[0, rows, cols].astype(F32) + gm_s[h, rows, :] * ym).astype(BF16)
        o = _dot(mg_s[rows, :], wout_ref[...])
        out_ref[0, rows, :] = x_ref[0, rows, :] + _rmsnorm(o, gpost_ref[...])


def _rope_tables(pos):
    half = ROPE_DIM // 2
    inv_freq = ROPE_THETA ** (-jnp.arange(0, ROPE_DIM, 2, dtype=F32) / ROPE_DIM)
    ang = pos[:, None] * inv_freq[None, :]
    cos, sin = jnp.cos(ang), jnp.sin(ang)
    ones = jnp.ones((pos.shape[0], HEAD_DIM - ROPE_DIM), F32)
    zeros_h = jnp.zeros((pos.shape[0], half), F32)
    zeros_r = jnp.zeros_like(ones)
    c_head = jnp.concatenate([cos, cos, ones], axis=1)
    s1_head = jnp.concatenate([-sin, zeros_h, zeros_r], axis=1)
    s2_head = jnp.concatenate([zeros_h, sin, zeros_r], axis=1)
    rep = LANES // HEAD_DIM
    return (jnp.tile(c_head, (1, rep)), jnp.tile(s1_head, (1, rep)), jnp.tile(s2_head, (1, rep)))


def _band_bias():
    kj = np.arange(2 * ATTN_BLOCK)[:, None]
    qi = np.arange(ATTN_BLOCK)[None, :]
    vis = (kj > qi) & (kj <= qi + ATTN_BLOCK)
    meta = np.zeros((N_META, ATTN_BLOCK), np.float32)
    bt = np.concatenate([np.where(vis, 0.0, NEG_BIG), meta], axis=0).astype(np.float32)
    bt0 = np.concatenate([np.where(vis & (kj >= ATTN_BLOCK), 0.0, NEG_BIG), meta], axis=0).astype(np.float32)
    eye2 = np.tile(np.eye(ATTN_BLOCK, dtype=np.float32), (1, 2))
    return jnp.asarray(bt, BF16), jnp.asarray(bt0, BF16), jnp.asarray(eye2, BF16)


def _full(shape, single=True):
    nd = len(shape)
    kw = {"pipeline_mode": pl.Buffered(1)} if single else {}
    return pl.BlockSpec(shape, lambda *_: (0,) * nd, **kw)


def kernel(x, meta_tokens, norm_pre, w_in, attn_sinks, conv_w, conv_b, mlstm_gate_bias, mlstm_head_norm,
           w_attn_out, w_mlstm_out, w_out, norm_post):
    B, S, D = x.shape
    assert D == D_MODEL and w_in.shape[0] == 1 and meta_tokens.shape == (N_META, D_MODEL)
    tile = min(SEQ_TILE, S)
    assert S % tile == 0 and tile % CHUNK == 0 and tile % ATTN_BLOCK == 0
    nt = S // tile
    nblk = tile // ATTN_BLOCK
    nchunk = tile // CHUNK

    w = w_in[0].astype(BF16)
    o = 0
    parts = {}
    for name, width in (("q", 1024), ("k", 256), ("v", 256), ("z", 1024), ("mqk", 1024), ("mv", 1024),
                        ("mi", 4), ("mf", 4), ("mo", 1024), ("mz", 1024), ("ga", 1024), ("gm", 1024)):
        parts[name] = w[:, o:o + width]
        o += width
    nslice = tile // ATTN_BLOCK
    assert nslice == nchunk == ML_HEADS and D // nslice == ML_DV

    def sliced(wm):
        return wm.reshape(D, nslice, wm.shape[1] // nslice).transpose(1, 0, 2)

    wvt = parts["v"].T
    wqt = parts["q"].T
    gate_rows = BF16_ROWS
    wmvt = jnp.concatenate([parts["mv"].T, parts["mi"].T, parts["mf"].T,
                            jnp.zeros((gate_rows - 2 * ML_HEADS, D), BF16)], axis=0)
    gbias = jnp.broadcast_to(jnp.concatenate([mlstm_gate_bias[0].astype(F32),
                                              jnp.zeros((gate_rows - 2 * ML_HEADS,), F32)])[:, None],
                             (gate_rows, CHUNK))
    gpre = norm_pre[0].astype(F32)[None, :]
    gpost = norm_post[0].astype(F32)[None, :]
    convw = conv_w[0].astype(F32)
    convb = conv_b[0].astype(F32)[None, :]
    hnorm = mlstm_head_norm[0].astype(F32)[None, :]
    wao = w_attn_out[0].astype(BF16)
    wmlo = w_mlstm_out[0].astype(BF16)
    wout = w_out[0].astype(BF16)
    sinks = attn_sinks[0].astype(F32)
    sink_rows = jnp.broadcast_to(jnp.repeat(sinks.reshape(N_HEADS // 2, 2), ATTN_BLOCK, axis=1)[:, None, :],
                                 (N_HEADS // 2, 8, 2 * ATTN_BLOCK))

    xm = jnp.concatenate([jnp.zeros((CHUNK - N_META, D), F32), meta_tokens.astype(F32)], axis=0)
    mpos = jnp.maximum(jnp.arange(CHUNK, dtype=F32) - (CHUNK - N_META), 0.0)
    mrc, mrs1, mrs2 = _rope_tables(mpos)
    kv_w = N_KV * HEAD_DIM
    meta_out = pl.pallas_call(
        _meta_kernel,
        out_shape=(jax.ShapeDtypeStruct((CHUNK, kv_w), F32), jax.ShapeDtypeStruct((CHUNK, kv_w), F32),
                   jax.ShapeDtypeStruct((CONV_HDR, D), F32),
                   jax.ShapeDtypeStruct((ML_HEADS // 2, ML_EXT, 2 * ML_DK), F32),
                   jax.ShapeDtypeStruct((8, LANES), F32)),
        scratch_shapes=[pltpu.VMEM((CONV_HDR + CHUNK, D), F32), pltpu.VMEM((CHUNK, D // 2), BF16),
                        pltpu.VMEM((CHUNK, D // 2), BF16), pltpu.VMEM((ML_HEADS // 2, 1, ML_EXT, 2 * CHUNK), F32),
                        pltpu.VMEM((1, 8, CHUNK), F32), pltpu.VMEM((1, 8, CHUNK), F32),
                        pltpu.VMEM((1, CHUNK, LANES), F32)],
        compiler_params=pltpu.CompilerParams(vmem_limit_bytes=VMEM_LIMIT),
        name="meta_tokens",
    )(xm, gpre, parts["k"], parts["v"], parts["mqk"], wmvt, convw, convb, gbias,
      mrc, mrs1, mrs2)
    kmeta, vmeta, tail0, c0, m0 = meta_out
    km = kmeta[CHUNK - N_META:].reshape(N_META, N_KV, HEAD_DIM).transpose(1, 0, 2)
    km2 = jnp.concatenate([km, km], axis=2).astype(BF16)
    vm = vmeta[CHUNK - N_META:].reshape(N_META, N_KV, HEAD_DIM).transpose(1, 2, 0)
    vmt = jnp.concatenate([vm, jnp.zeros((N_KV, HEAD_DIM, LANES - N_META), F32)], axis=2).astype(BF16)

    rc, rs1, rs2 = _rope_tables(jnp.arange(S, dtype=F32) + N_META)
    rct, rst = rc[:, 0:ROPE_DIM // 2].T, rs2[:, ROPE_DIM // 2:ROPE_DIM].T
    bt, bt0, eye2 = _band_bias()
    x_spec = pl.BlockSpec((1, tile, D), lambda b, j: (b, j, 0))
    seq_params = pltpu.CompilerParams(dimension_semantics=("arbitrary", "arbitrary"),
                                      vmem_limit_bytes=VMEM_LIMIT)
    assert nt % 2 == 0
    x2_spec = pl.BlockSpec((1, 2 * tile, D), lambda b, j: (b, j, 0))
    rope_spec = pl.BlockSpec((2 * tile, LANES), lambda b, j: (j, 0))
    rope_t_spec = pl.BlockSpec((ROPE_DIM // 2, 2 * tile), lambda b, j: (0, j))
    ya = pl.pallas_call(
        _attn_kernel,
        grid=(B, nt // 2),
        in_specs=[x2_spec, _full((1, D)),
                  _full((D, 1024)), _full((D, kv_w)), _full((kv_w, D)), _full((nslice, D, ML_DV)),
                  _full((nslice, D, ML_DV)), _full((D, D)),
                  rope_spec, rope_spec, rope_spec, rope_t_spec, rope_t_spec,
                  _full((N_KV, N_META, LANES)), _full((N_KV, HEAD_DIM, LANES)),
                  _full((BAND, ATTN_BLOCK)), _full((BAND, ATTN_BLOCK)), _full((ATTN_BLOCK, 2 * ATTN_BLOCK)),
                  _full((N_HEADS // 2, 8, 2 * ATTN_BLOCK))],
        out_specs=pl.BlockSpec((1, 2 * tile, D), lambda b, j: (b, j, 0)),
        out_shape=jax.ShapeDtypeStruct((B, S, D), BF16),
        scratch_shapes=[pltpu.VMEM((tile, D), BF16),
                        pltpu.VMEM((N_HEADS // 2, nblk, 2 * HEAD_DIM, 2 * ATTN_BLOCK), BF16),
                        pltpu.VMEM((N_KV, ATTN_BLOCK + tile, LANES), BF16),
                        pltpu.VMEM((N_KV, nblk + 1, HEAD_DIM, ATTN_BLOCK), BF16),
                        pltpu.VMEM((N_HEADS // 2, tile, LANES), F32),
                        pltpu.VMEM((tile, D), BF16),
                        pltpu.VMEM((nslice, tile, ML_DV), BF16),
                        pltpu.VMEM((nslice, tile, ML_DV), BF16)],
        compiler_params=seq_params,
        name="attn_branch",
    )(x, gpre, wqt, parts["k"], wvt, sliced(parts["z"]), sliced(parts["ga"]), wao,
      rc, rs1, rs2, rct, rst, km2, vmt, bt, bt0, eye2, sink_rows)

    out = pl.pallas_call(
        _mlstm_kernel,
        grid=(B, nt),
        in_specs=[x_spec, pl.BlockSpec((1, tile, D), lambda b, j: (b, j, 0)),
                  _full((1, D)), _full((1, D)),
                  _full((D, 1024)), _full((1024 + gate_rows, D)), _full((nslice, D, ML_DV)),
                  _full((nslice, D, ML_DV)), _full((nslice, D, ML_DV)), _full((D, D)), _full((D, D)),
                  _full((CONV_W, D)), _full((1, D)), _full((gate_rows, CHUNK)), _full((1, D)),
                  _full((CONV_HDR, D)), _full((ML_HEADS // 2, ML_EXT, 2 * ML_DK)), _full((8, LANES))],
        out_specs=pl.BlockSpec((1, tile, D), lambda b, j: (b, j, 0)),
        out_shape=jax.ShapeDtypeStruct((B, S, D), x.dtype),
        scratch_shapes=[pltpu.VMEM((tile, D), BF16),
                        pltpu.VMEM((CONV_HDR + tile, D), F32),
                        pltpu.VMEM((tile, D // 2), BF16),
                        pltpu.VMEM((tile, D // 2), BF16),
                        pltpu.VMEM((ML_HEADS // 2, nchunk, ML_EXT, 2 * CHUNK), F32),
                        pltpu.VMEM((nchunk, 8, CHUNK), F32),
                        pltpu.VMEM((nchunk, 8, CHUNK), F32),
                        pltpu.VMEM((nchunk, CHUNK, LANES), F32),
                        pltpu.VMEM((tile, D), F32),
                        pltpu.VMEM((tile, D), BF16),
                        pltpu.VMEM((tile, D), BF16),
                        pltpu.VMEM((ML_HEADS // 2, ML_EXT, 2 * ML_DK), F32),
                        pltpu.VMEM((8, LANES), F32),
                        pltpu.VMEM((nslice, tile, ML_DV), F32),
                        pltpu.VMEM((nslice, tile, ML_DV), F32),
                        pltpu.VMEM((nslice, tile, ML_DV), F32)],
        compiler_params=seq_params,
        name="mlstm_merge_out",
    )(x, ya, gpre, gpost, parts["mqk"], wmvt, sliced(parts["mo"]), sliced(parts["mz"]),
      sliced(parts["gm"]),
      wmlo, wout, convw, convb, gbias, hnorm, tail0, c0, m0)
    return out
```
